```python
import math, functools
import jax, jax.numpy as jnp
from jax import lax
import numpy as np

D_MODEL = 2048
BATCH = 1
SEQ = 8192
DEPTH = 2
DEC_BATCH = 128
DEC_SEQ = 1
PAST_LEN = 2048
PAGE_SIZE = 128

N_A_LAYERS = DEPTH // 2
N_B_LAYERS = DEPTH - N_A_LAYERS
HEAD_DIM = 128
N_HEADS = D_MODEL // HEAD_DIM
ATTN_W = N_HEADS * HEAD_DIM
CONV_W = 31
D_FF = ((8 * D_MODEL // 3 + 255) // 256) * 256
MOBA_BLOCK = 256
MOBA_TOPK = 3
Q_CHUNK = 128
ROPE_THETA = 10000.0
LN_EPS = 1e-5
ATTN_SCALE = HEAD_DIM ** -0.5
DN_ALPHA = (2.0 * DEPTH) ** 0.25
DN_BETA = (8.0 * DEPTH) ** -0.25

kernel_name = 'yoco_conformer_conv_moba_deepnorm_step'


def layer_norm(x, g, b):
    xf = x.astype(jnp.float32)
    mu = jnp.mean(xf, -1, keepdims=True)
    var = jnp.mean(jnp.square(xf - mu), -1, keepdims=True)
    return ((xf - mu) * lax.rsqrt(var + LN_EPS)).astype(x.dtype) * g + b


def post_norm(x, f, g, b):
    return layer_norm(DN_ALPHA * x + f, g, b)


def swiglu(x, w_gate, w_up, w_down):
    return (jax.nn.silu(x @ w_gate) * (x @ w_up)) @ w_down


def rope(x, pos):
    half = HEAD_DIM // 2
    inv = ROPE_THETA ** (-jnp.arange(half, dtype=jnp.float32) / half)
    ang = pos.astype(jnp.float32)[:, None] * inv[None, :]
    cos = jnp.cos(ang)[:, None, :]
    sin = jnp.sin(ang)[:, None, :]
    xf = x.astype(jnp.float32)
    x1, x2 = xf[..., :half], xf[..., half:]
    return jnp.concatenate([x1 * cos - x2 * sin, x2 * cos + x1 * sin], -1).astype(x.dtype)


def conv_module(x, conv_state, w_pw1, b_pw1, w_dw, b_dw, g_cn, b_cn, w_pw2, b_pw2):
    a = x @ w_pw1 + b_pw1
    u = a[..., :D_MODEL] * jax.nn.sigmoid(a[..., D_MODEL:])
    ext = jnp.concatenate([conv_state.astype(u.dtype), u], axis=1)
    y = lax.conv_general_dilated(ext, w_dw[:, None, :], window_strides=(1,), padding='VALID',
                                 dimension_numbers=('NWC', 'WIO', 'NWC'),
                                 feature_group_count=D_MODEL) + b_dw
    y = jax.nn.silu(layer_norm(y, g_cn, b_cn))
    return y @ w_pw2 + b_pw2, ext[:, ext.shape[1] - (CONV_W - 1):]


def softmax_merge(s_own, v_own, s_sel=None, v_sel=None):
    if s_sel is None:
        p = jax.nn.softmax(s_own, -1).astype(v_own.dtype)
        return jnp.einsum('bhcl,bhld->bhcd', p, v_own)
    m = s_sel.shape[-1]
    p = jax.nn.softmax(jnp.concatenate([s_sel, s_own], -1), -1).astype(v_own.dtype)
    return (jnp.einsum('bhcm,bhcmd->bhcd', p[..., :m], v_sel)
            + jnp.einsum('bhcl,bhld->bhcd', p[..., m:], v_own))


def moba_prompt(q, k, v):
    B, S = q.shape[0], q.shape[1]
    nb = -(-S // MOBA_BLOCK)
    pad = nb * MOBA_BLOCK - S
    def blocks(t):
        t = jnp.pad(t, ((0, 0), (0, pad), (0, 0), (0, 0)))
        return t.reshape(B, nb, MOBA_BLOCK, N_HEADS, HEAD_DIM).transpose(0, 3, 1, 2, 4)
    kb, vb = blocks(k), blocks(v)
    qf = q.transpose(0, 2, 1, 3)
    qblock = jnp.arange(S) // MOBA_BLOCK
    ksel = min(MOBA_TOPK, nb - 1)
    if ksel > 0:
        kmean = jnp.mean(kb.astype(jnp.float32), axis=3)
        gate = jnp.einsum('bhsd,bhnd->bhsn', qf.astype(jnp.float32), kmean)
        past = jnp.arange(nb)[None, :] < qblock[:, None]
        gate = jnp.where(past, gate, -jnp.inf)
        _, sel = lax.top_k(gate, ksel)
        sel_ok = sel < qblock[None, None, :, None]
    bidx = jnp.arange(B)[:, None, None, None]
    hidx = jnp.arange(N_HEADS)[None, :, None, None]

    def chunk(c):
        s0 = c * Q_CHUNK
        qc = lax.dynamic_slice_in_dim(qf, s0, Q_CHUNK, axis=2)
        ib = s0 // MOBA_BLOCK
        k_own = lax.dynamic_index_in_dim(kb, ib, axis=2, keepdims=False)
        v_own = lax.dynamic_index_in_dim(vb, ib, axis=2, keepdims=False)
        qpos = s0 + jnp.arange(Q_CHUNK)
        kpos = ib * MOBA_BLOCK + jnp.arange(MOBA_BLOCK)
        s_own = jnp.einsum('bhcd,bhld->bhcl', qc, k_own).astype(jnp.float32) * ATTN_SCALE
        s_own = jnp.where(kpos[None, :] <= qpos[:, None], s_own, -jnp.inf)
        if ksel == 0:
            return softmax_merge(s_own, v_own)
        sc = lax.dynamic_slice_in_dim(sel, s0, Q_CHUNK, axis=2)
        okc = lax.dynamic_slice_in_dim(sel_ok, s0, Q_CHUNK, axis=2)
        k_sel = kb[bidx, hidx, sc]
        v_sel = vb[bidx, hidx, sc]
        s_sel = jnp.einsum('bhcd,bhcnld->bhcnl', qc, k_sel).astype(jnp.float32) * ATTN_SCALE
        s_sel = jnp.where(okc[..., None], s_sel, -jnp.inf).reshape(B, N_HEADS, Q_CHUNK, ksel * MOBA_BLOCK)
        v_sel = v_sel.reshape(B, N_HEADS, Q_CHUNK, ksel * MOBA_BLOCK, HEAD_DIM)
        return softmax_merge(s_own, v_own, s_sel, v_sel)

    out = lax.map(chunk, jnp.arange(S // Q_CHUNK))
    return out.transpose(1, 0, 3, 2, 4).reshape(B, S, ATTN_W)


def moba_sample(q, k_new, v_new, cache_k, cache_v, page_table):
    Bd, Sd = q.shape[0], q.shape[1]
    ppb = MOBA_BLOCK // PAGE_SIZE
    n_pages = PAST_LEN // PAGE_SIZE
    cur = PAST_LEN // MOBA_BLOCK
    own_len = PAST_LEN - cur * MOBA_BLOCK
    qf = q.transpose(0, 2, 1, 3)
    own_pages = page_table[:, cur * ppb:n_pages]
    k_own = jnp.concatenate([cache_k[own_pages].reshape(Bd, own_len, N_HEADS, HEAD_DIM).astype(k_new.dtype), k_new], 1)
    v_own = jnp.concatenate([cache_v[own_pages].reshape(Bd, own_len, N_HEADS, HEAD_DIM).astype(v_new.dtype), v_new], 1)
    k_own = k_own.transpose(0, 2, 1, 3)
    v_own = v_own.transpose(0, 2, 1, 3)
    s_own = jnp.einsum('bhcd,bhld->bhcl', qf, k_own).astype(jnp.float32) * ATTN_SCALE
    mask = jnp.arange(own_len + Sd)[None, :] <= own_len + jnp.arange(Sd)[:, None]
    s_own = jnp.where(mask, s_own, -jnp.inf)
    ksel = min(MOBA_TOPK, cur)
    if ksel == 0:
        out = softmax_merge(s_own, v_own)
    else:
        block_pages = page_table[:, :cur * ppb].reshape(Bd, cur, ppb)
        kmean = jnp.mean(cache_k[block_pages].astype(jnp.float32), axis=(2, 3))
        gate = jnp.einsum('bhsd,bnhd->bhsn', qf.astype(jnp.float32), kmean)
        _, sel = lax.top_k(gate, ksel)
        bidx = jnp.arange(Bd)[:, None, None, None]
        pages_sel = block_pages[bidx, sel]
        hidx = jnp.arange(N_HEADS)[None, :, None, None, None]
        m = ksel * ppb * PAGE_SIZE
        k_sel = cache_k[pages_sel, :, hidx].reshape(Bd, N_HEADS, Sd, m, HEAD_DIM).astype(k_new.dtype)
        v_sel = cache_v[pages_sel, :, hidx].reshape(Bd, N_HEADS, Sd, m, HEAD_DIM).astype(v_new.dtype)
        s_sel = jnp.einsum('bhcd,bhcmd->bhcm', qf, k_sel).astype(jnp.float32) * ATTN_SCALE
        out = softmax_merge(s_own, v_own, s_sel, v_sel)
    return out.transpose(0, 2, 1, 3).reshape(Bd, Sd, ATTN_W)


def trunk(x, pos, conv_states, attend, ffn_w_gate, ffn_w_up, ffn_w_down, ln_g, ln_b,
          conv_w_pw1, conv_b_pw1, conv_w_dw, conv_b_dw, conv_ln_g, conv_ln_b, conv_w_pw2, conv_b_pw2,
          attn_w_q, attn_w_o, w_kv):
    B, S = x.shape[0], x.shape[1]
    new_conv = []
    k_sh = v_sh = None
    for layer in range(DEPTH):
        x = post_norm(x, 0.5 * swiglu(x, ffn_w_gate[layer, 0], ffn_w_up[layer, 0], ffn_w_down[layer, 0]),
                      ln_g[layer, 0], ln_b[layer, 0])
        if layer < N_A_LAYERS:
            i = layer
            m, st = conv_module(x, conv_states[i], conv_w_pw1[i], conv_b_pw1[i], conv_w_dw[i], conv_b_dw[i],
                                conv_ln_g[i], conv_ln_b[i], conv_w_pw2[i], conv_b_pw2[i])
            new_conv.append(st)
        else:
            j = layer - N_A_LAYERS
            q = rope((x @ attn_w_q[j]).reshape(B, S, N_HEADS, HEAD_DIM), pos)
            m = attend(q, k_sh, v_sh) @ attn_w_o[j]
        x = post_norm(x, m, ln_g[layer, 1], ln_b[layer, 1])
        x = post_norm(x, 0.5 * swiglu(x, ffn_w_gate[layer, 1], ffn_w_up[layer, 1], ffn_w_down[layer, 1]),
                      ln_g[layer, 2], ln_b[layer, 2])
        if layer == N_A_LAYERS - 1:
            kv = x @ w_kv
            k_sh = rope(kv[..., :ATTN_W].reshape(B, S, N_HEADS, HEAD_DIM), pos)
            v_sh = kv[..., ATTN_W:].reshape(B, S, N_HEADS, HEAD_DIM)
    return x, jnp.stack(new_conv), k_sh, v_sh


def setup_inputs(seed: int = 0) -> dict:
    key = jax.random.key(seed)
    ks = jax.random.split(key, 32)
    n_pages = PAST_LEN // PAGE_SIZE
    n_phys = (DEC_BATCH * n_pages * 5) // 4
    D = D_MODEL
    def nrm(k, shape, scale):
        return jax.random.normal(k, shape, jnp.float32) * scale
    page_table = jax.random.permutation(ks[5], n_phys)[:DEC_BATCH * n_pages].reshape(DEC_BATCH, n_pages).astype(jnp.int32)
    w_k = nrm(ks[24], (D, ATTN_W), D ** -0.5)
    w_v = nrm(ks[25], (D, ATTN_W), D ** -0.5 * DN_BETA)
    return {
        'x_prompt': nrm(ks[0], (BATCH, SEQ, D), 1.0),
        'x_sample': nrm(ks[1], (DEC_BATCH, DEC_SEQ, D), 1.0),
        'state_conv': nrm(ks[2], (N_A_LAYERS, DEC_BATCH, CONV_W - 1, D), 0.5),
        'cache_k': nrm(ks[3], (n_phys, PAGE_SIZE, N_HEADS, HEAD_DIM), 1.0),
        'cache_v': nrm(ks[4], (n_phys, PAGE_SIZE, N_HEADS, HEAD_DIM), 1.0),
        'page_table': page_table,
        'ffn_w_gate': nrm(ks[6], (DEPTH, 2, D, D_FF), D ** -0.5),
        'ffn_w_up': nrm(ks[7], (DEPTH, 2, D, D_FF), D ** -0.5),
        'ffn_w_down': nrm(ks[8], (DEPTH, 2, D_FF, D), D_FF ** -0.5 * DN_BETA),
        'ln_g': 1.0 + nrm(ks[9], (DEPTH, 3, D), 0.02),
        'ln_b': nrm(ks[10], (DEPTH, 3, D), 0.02),
        'conv_w_pw1': nrm(ks[11], (N_A_LAYERS, D, 2 * D), D ** -0.5),
        'conv_b_pw1': nrm(ks[12], (N_A_LAYERS, 2 * D), 0.02),
        'conv_w_dw': nrm(ks[13], (N_A_LAYERS, CONV_W, D), CONV_W ** -0.5),
        'conv_b_dw': nrm(ks[14], (N_A_LAYERS, D), 0.02),
        'conv_ln_g': 1.0 + nrm(ks[15], (N_A_LAYERS, D), 0.02),
        'conv_ln_b': nrm(ks[16], (N_A_LAYERS, D), 0.02),
        'conv_w_pw2': nrm(ks[17], (N_A_LAYERS, D, D), D ** -0.5 * DN_BETA),
        'conv_b_pw2': nrm(ks[18], (N_A_LAYERS, D), 0.02),
        'attn_w_q': nrm(ks[19], (N_B_LAYERS, D, ATTN_W), D ** -0.5),
        'attn_w_o': nrm(ks[20], (N_B_LAYERS, ATTN_W, D), ATTN_W ** -0.5 * DN_BETA),
        'w_kv': jnp.concatenate([w_k, w_v], axis=1),
    }


def reference(x_prompt, x_sample, state_conv, cache_k, cache_v, page_table,
              ffn_w_gate, ffn_w_up, ffn_w_down, ln_g, ln_b,
              conv_w_pw1, conv_b_pw1, conv_w_dw, conv_b_dw, conv_ln_g, conv_ln_b, conv_w_pw2, conv_b_pw2,
              attn_w_q, attn_w_o, w_kv):
    weights = (ffn_w_gate, ffn_w_up, ffn_w_down, ln_g, ln_b,
               conv_w_pw1, conv_b_pw1, conv_w_dw, conv_b_dw, conv_ln_g, conv_ln_b, conv_w_pw2, conv_b_pw2,
               attn_w_q, attn_w_o, w_kv)
    pos_p = jnp.arange(SEQ)
    conv0 = jnp.zeros((N_A_LAYERS, BATCH, CONV_W - 1, D_MODEL), x_prompt.dtype)
    y_prompt, conv_p, k_p, v_p = trunk(x_prompt, pos_p, conv0, moba_prompt, *weights)
    pos_s = PAST_LEN + jnp.arange(DEC_SEQ)
    attend_s = lambda q, k, v: moba_sample(q, k, v, cache_k, cache_v, page_table)
    y_sample, conv_s, k_s, v_s = trunk(x_sample, pos_s, state_conv, attend_s, *weights)
    return (y_prompt, y_sample, conv_p, conv_s, k_p, v_p, k_s, v_s)
```

```python
import functools
import math

import jax
import jax.numpy as jnp
from jax import lax
from jax.experimental import pallas as pl
from jax.experimental.pallas import tpu as pltpu

_F32 = jnp.float32
_BF16 = jnp.bfloat16

MOBA_BLOCK = 256
MOBA_TOPK = 3
ROPE_THETA = 10000.0
LN_EPS = 1e-5
LANES = 128
MASK_BIAS = -1e30
V7X_VMEM_LIMIT = 56 * 1024 * 1024

_NT = (((1,), (1,)), ((), ()))


def _params(n_grid_dims, vmem_bytes=V7X_VMEM_LIMIT):
    return pltpu.CompilerParams(
        dimension_semantics=("arbitrary",) * n_grid_dims, vmem_limit_bytes=vmem_bytes)


def _largest_tile(n, candidates):
    for c in candidates:
        if n % c == 0:
            return c
    return n


def _layer_norm(z, g, b):
    mu = jnp.mean(z, axis=-1, keepdims=True)
    zc = z - mu
    var = jnp.mean(zc * zc, axis=-1, keepdims=True)
    return zc * lax.rsqrt(var + LN_EPS) * g + b


def _sigmoid(a):
    return 1.0 / (1.0 + jnp.exp(-a))


def _dual(i, n_prompt_tiles, prompt_fn, sample_fn):
    pl.when(i < n_prompt_tiles)(prompt_fn)
    pl.when(i == n_prompt_tiles)(sample_fn)


def _prompt_rows(npt):
    return lambda i, *_: (jnp.minimum(i, npt - 1), 0)


def _prompt_tile(npt, nj):
    return lambda i, j: (jnp.minimum(i, npt - 1), jnp.where(i < npt, j, nj - 1))


def _sample_tile(npt):
    return lambda i, j: (0, jnp.where(i < npt, 0, j))


def _ffn_kernel(xp_ref, xs_ref, wg_ref, wu_ref, wd_ref, g_ref, b_ref, op_ref, os_ref, xb_ref,
                *, npt, ns, alpha):
    i, j, nj = pl.program_id(0), pl.program_id(1), pl.num_programs(1)

    def body(x_ref, o_ref, xb):
        @pl.when(j == 0)
        def _():
            xb[...] = x_ref[...].astype(_BF16)
            o_ref[...] = jnp.zeros_like(o_ref)

        x16 = xb[...]
        a = jnp.dot(x16, wg_ref[...], preferred_element_type=_F32)
        u = jnp.dot(x16, wu_ref[...], preferred_element_type=_F32)
        h = (a * _sigmoid(a)) * u
        o_ref[...] += jnp.dot(h.astype(_BF16), wd_ref[...], preferred_element_type=_F32)

        @pl.when(j == nj - 1)
        def _():
            z = alpha * x_ref[...] + 0.5 * o_ref[...]
            o_ref[...] = _layer_norm(z, g_ref[...], b_ref[...])

    _dual(i, npt, lambda: body(xp_ref, op_ref, xb_ref),
          lambda: body(xs_ref, os_ref, xb_ref.at[pl.ds(0, ns)]))


def _ffn(xp, xs, wg, wu, wd, g4, b4, layer, which, alpha, tm):
    norm = 2 * which
    S, D = xp.shape
    ns = xs.shape[0]
    F = wg.shape[-1]
    tf = _largest_tile(F, (512, 256, 128))
    npt = S // tm
    w_in = pl.BlockSpec((None, None, D, tf), lambda i, j: (layer, which, 0, j))
    w_out = pl.BlockSpec((None, None, tf, D), lambda i, j: (layer, which, j, 0))
    vec = pl.BlockSpec((None, None, 1, D), lambda i, j: (layer, norm, 0, 0))
    return pl.pallas_call(
        functools.partial(_ffn_kernel, npt=npt, ns=ns, alpha=alpha),
        grid=(npt + 1, F // tf),
        in_specs=[pl.BlockSpec((tm, D), _prompt_rows(npt)),
                  pl.BlockSpec((ns, D), lambda i, j: (0, 0)),
                  w_in, w_in, w_out, vec, vec],
        out_specs=[pl.BlockSpec((tm, D), _prompt_rows(npt)),
                   pl.BlockSpec((ns, D), lambda i, j: (0, 0))],
        out_shape=[jax.ShapeDtypeStruct((S, D), _F32), jax.ShapeDtypeStruct((ns, D), _F32)],
        scratch_shapes=[pltpu.VMEM((tm, D), _BF16)],
        compiler_params=_params(2),
        name="ffn_postnorm",
    )(xp, xs, wg, wu, wd, g4, b4)


def _glu_kernel(xp_ref, xs_ref, wa_ref, wb_ref, ba_ref, bb_ref, op_ref, os_ref, xb_ref,
                *, npt, ns):
    i, j = pl.program_id(0), pl.program_id(1)

    def body(x_ref, o_ref, xb):
        @pl.when(j == 0)
        def _():
            xb[...] = x_ref[...].astype(_BF16)

        x16 = xb[...]
        a = jnp.dot(x16, wa_ref[...], preferred_element_type=_F32) + ba_ref[...]
        gate = jnp.dot(x16, wb_ref[...], preferred_element_type=_F32) + bb_ref[...]
        o_ref[...] = a * _sigmoid(gate)

    _dual(i, npt, lambda: body(xp_ref, op_ref, xb_ref),
          lambda: body(xs_ref, os_ref, xb_ref.at[pl.ds(0, ns)]))


def _glu(xp, xs, w_pw1, b_pw1, tm):
    S, D = xp.shape
    ns = xs.shape[0]
    tn = _largest_tile(D, (512, 256, 128))
    npt, nj = S // tm, D // tn
    b3 = b_pw1.reshape(1, 1, 2 * D)
    return pl.pallas_call(
        functools.partial(_glu_kernel, npt=npt, ns=ns),
        grid=(npt + 1, nj),
        in_specs=[pl.BlockSpec((tm, D), _prompt_rows(npt)),
                  pl.BlockSpec((ns, D), lambda i, j: (0, 0)),
                  pl.BlockSpec((None, D, tn), lambda i, j: (0, 0, j)),
                  pl.BlockSpec((None, D, tn), lambda i, j: (0, 0, j + nj)),
                  pl.BlockSpec((None, 1, tn), lambda i, j: (0, 0, j)),
                  pl.BlockSpec((None, 1, tn), lambda i, j: (0, 0, j + nj))],
        out_specs=[pl.BlockSpec((tm, tn), _prompt_tile(npt, nj)),
                   pl.BlockSpec((ns, tn), _sample_tile(npt))],
        out_shape=[jax.ShapeDtypeStruct((S, D), _F32), jax.ShapeDtypeStruct((ns, D), _F32)],
        scratch_shapes=[pltpu.VMEM((tm, D), _BF16)],
        compiler_params=_params(2),
        name="pw1_glu",
    )(xp, xs, w_pw1, w_pw1, b3, b3)


def _conv_tail(y, x1, gcn_ref, bcn_ref, w2_ref, b2_ref, g_ref, b_ref, alpha):
    z = _layer_norm(y, gcn_ref[...], bcn_ref[...])
    z = z * _sigmoid(z)
    m = jnp.dot(z.astype(_BF16), w2_ref[...], preferred_element_type=_F32) + b2_ref[...]
    return _layer_norm(alpha * x1 + m, g_ref[...], b_ref[...])


CONV_HALO = 32
CONV_ROWS = 32
CONV_COLS = 512


def _conv_prompt_kernel(u_ref, halo_ref, x1_ref, wdw_ref, bdw_ref, gcn_ref, bcn_ref, w2_ref,
                        b2_ref, g_ref, b_ref, o_ref, ext_ref, y_ref, *, conv_w, alpha):
    i = pl.program_id(0)
    tm, D = u_ref.shape
    ext_ref[0:CONV_HALO, :] = jnp.where(i == 0, 0.0, halo_ref[...])
    ext_ref[CONV_HALO:, :] = u_ref[...]
    lead = CONV_HALO - (conv_w - 1)
    cc = min(CONV_COLS, D)
    for r in range(tm // CONV_ROWS):
        r0 = r * CONV_ROWS
        for c in range(D // cc):
            cols = slice(c * cc, (c + 1) * cc)
            win = ext_ref[r0:r0 + CONV_ROWS + CONV_HALO, cols]
            acc = jnp.zeros((CONV_ROWS, cc), _F32)
            for k in range(conv_w):
                acc = acc + wdw_ref[k:k + 1, cols] * win[lead + k:lead + k + CONV_ROWS, :]
            y_ref[r0:r0 + CONV_ROWS, cols] = acc + bdw_ref[:, cols]
    o_ref[...] = _conv_tail(y_ref[...], x1_ref[...], gcn_ref, bcn_ref, w2_ref, b2_ref,
                            g_ref, b_ref, alpha)


def _conv_sample_kernel(st_ref, u_ref, x1_ref, wdw_ref, bdw_ref, gcn_ref, bcn_ref, w2_ref,
                        b2_ref, g_ref, b_ref, o_ref, *, conv_w, alpha):
    w = wdw_ref[...]
    y = jnp.sum(st_ref[...] * w[None, 0:conv_w - 1, :], axis=1)
    y = y + u_ref[...] * w[conv_w - 1:conv_w, :] + bdw_ref[...]
    o_ref[...] = _conv_tail(y, x1_ref[...], gcn_ref, bcn_ref, w2_ref, b2_ref, g_ref, b_ref,
                            alpha)


def _conv_mixer(up, us, state, x1p, x1s, wdw, bdw, gcn, bcn, w2, b2, g, b, alpha):
    S, D = up.shape
    ns = us.shape[0]
    conv_w = wdw.shape[0]
    assert conv_w - 1 <= CONV_HALO
    tm = _largest_tile(S, (256, 128))
    per_halo = tm // CONV_HALO
    full = lambda shape: pl.BlockSpec(shape, lambda i: (0,) * len(shape))
    shared = [full((conv_w, D)), full((1, D)), full((1, D)), full((1, D)), full((D, D)),
              full((1, D)), full((1, D)), full((1, D))]
    shared_args = (wdw, bdw, gcn, bcn, w2, b2, g, b)
    xp = pl.pallas_call(
        functools.partial(_conv_prompt_kernel, conv_w=conv_w, alpha=alpha),
        grid=(S // tm,),
        in_specs=[pl.BlockSpec((tm, D), lambda i: (i, 0)),
                  pl.BlockSpec((CONV_HALO, D), lambda i: (jnp.maximum(i * per_halo - 1, 0), 0)),
                  pl.BlockSpec((tm, D), lambda i: (i, 0))] + shared,
        out_specs=pl.BlockSpec((tm, D), lambda i: (i, 0)),
        out_shape=jax.ShapeDtypeStruct((S, D), _F32),
        scratch_shapes=[pltpu.VMEM((CONV_HALO + tm, D), _F32), pltpu.VMEM((tm, D), _F32)],
        compiler_params=_params(1),
        name="conv_prompt",
    )(up, up, x1p, *shared_args)
    tb = _largest_tile(ns, (16, 8))
    xs = pl.pallas_call(
        functools.partial(_conv_sample_kernel, conv_w=conv_w, alpha=alpha),
        grid=(ns // tb,),
        in_specs=[pl.BlockSpec((tb, conv_w - 1, D), lambda i: (i, 0, 0)),
                  pl.BlockSpec((tb, D), lambda i: (i, 0)),
                  pl.BlockSpec((tb, D), lambda i: (i, 0))] + shared,
        out_specs=pl.BlockSpec((tb, D), lambda i: (i, 0)),
        out_shape=jax.ShapeDtypeStruct((ns, D), _F32),
        compiler_params=_params(1),
        name="conv_sample",
    )(state, us, x1s, *shared_args)
    return xp, xs


def _rope_tables(pos, inv_ref, cos_ref, sin_ref):
    ang = pos * inv_ref[...]
    lane = lax.broadcasted_iota(jnp.int32, ang.shape, 1)
    cos_ref[...] = jnp.cos(ang)
    sin_ref[...] = jnp.where(lane < LANES // 2, -1.0, 1.0) * jnp.sin(ang)


def _rope_head(x, cos, sin):
    return x * cos + pltpu.roll(x, LANES // 2, axis=1) * sin


def _row_positions(i, tm):
    return (i * tm + lax.broadcasted_iota(jnp.int32, (tm, LANES), 0)).astype(_F32)


def _kv_kernel(xp_ref, xs_ref, wk_ref, wv_ref, inv_ref,
               kp_ref, vp_ref, kbp_ref, vbp_ref, km_ref, ks_ref, vs_ref,
               xb_ref, cos_ref, sin_ref, *, npt, ns, past_len):
    i, j = pl.program_id(0), pl.program_id(1)
    tn = wk_ref.shape[1]

    def body(x_ref, xb, cos_r, sin_r, pos_fn, k_ref, v_ref, kb_ref, vb_ref, kmean_ref):
        rows = x_ref.shape[0]

        @pl.when(j == 0)
        def _():
            xb[...] = x_ref[...].astype(_BF16)
            _rope_tables(pos_fn(rows), inv_ref, cos_r, sin_r)

        x16 = xb[...]
        k = jnp.dot(x16, wk_ref[...], preferred_element_type=_F32)
        v = jnp.dot(x16, wv_ref[...], preferred_element_type=_F32)
        cos, sin = cos_r[...], sin_r[...]
        k = jnp.concatenate(
            [_rope_head(k[:, h * LANES:(h + 1) * LANES], cos, sin) for h in range(tn // LANES)],
            axis=1)
        k_ref[...] = k
        v_ref[...] = v
        if kb_ref is not None:
            kb_ref[...] = k.astype(_BF16)
            vb_ref[...] = v.astype(_BF16)
            nblk = rows // MOBA_BLOCK
            means = [jnp.mean(k[b * MOBA_BLOCK:(b + 1) * MOBA_BLOCK], axis=0, keepdims=True)
                     for b in range(nblk)]
            means.append(jnp.zeros((kmean_ref.shape[0] - nblk, tn), _F32))
            kmean_ref[...] = jnp.concatenate(means, axis=0)

    _dual(i, npt,
          lambda: body(xp_ref, xb_ref, cos_ref, sin_ref, lambda rows: _row_positions(i, rows),
                       kp_ref, vp_ref, kbp_ref, vbp_ref, km_ref),
          lambda: body(xs_ref, xb_ref.at[pl.ds(0, ns)], cos_ref.at[pl.ds(0, ns)],
                       sin_ref.at[pl.ds(0, ns)],
                       lambda rows: jnp.full((rows, LANES), float(past_len), _F32),
                       ks_ref, vs_ref, None, None, None))


KMEAN_ROWS = 8


def _kv(xp, xs, w_kv, inv128, past_len, tm):
    S, D = xp.shape
    ns = xs.shape[0]
    AW = w_kv.shape[1] // 2
    tn = _largest_tile(AW, (512, 256, 128))
    npt, nj = S // tm, AW // tn
    assert tm % MOBA_BLOCK == 0 and tm // MOBA_BLOCK <= KMEAN_ROWS
    prow, srow = _prompt_tile(npt, nj), _sample_tile(npt)
    outs = pl.pallas_call(
        functools.partial(_kv_kernel, npt=npt, ns=ns, past_len=past_len),
        grid=(npt + 1, nj),
        in_specs=[pl.BlockSpec((tm, D), _prompt_rows(npt)),
                  pl.BlockSpec((ns, D), lambda i, j: (0, 0)),
                  pl.BlockSpec((D, tn), lambda i, j: (0, j)),
                  pl.BlockSpec((D, tn), lambda i, j: (0, j + nj)),
                  pl.BlockSpec((1, LANES), lambda i, j: (0, 0))],
        out_specs=[pl.BlockSpec((tm, tn), prow), pl.BlockSpec((tm, tn), prow),
                   pl.BlockSpec((tm, tn), prow), pl.BlockSpec((tm, tn), prow),
                   pl.BlockSpec((None, KMEAN_ROWS, tn),
                                lambda i, j: (prow(i, j)[0], 0, prow(i, j)[1])),
                   pl.BlockSpec((ns, tn), srow), pl.BlockSpec((ns, tn), srow)],
        out_shape=[jax.ShapeDtypeStruct((S, AW), _F32), jax.ShapeDtypeStruct((S, AW), _F32),
                   jax.ShapeDtypeStruct((S, AW), _BF16), jax.ShapeDtypeStruct((S, AW), _BF16),
                   jax.ShapeDtypeStruct((npt, KMEAN_ROWS, AW), _F32),
                   jax.ShapeDtypeStruct((ns, AW), _F32), jax.ShapeDtypeStruct((ns, AW), _F32)],
        scratch_shapes=[pltpu.VMEM((tm, D), _BF16), pltpu.VMEM((tm, LANES), _F32),
                        pltpu.VMEM((tm, LANES), _F32)],
        compiler_params=_params(2),
        name="kv_rope",
    )(xp, xs, w_kv, w_kv, inv128)
    kp, vp, kbp, vbp, km3, ks, vs = outs
    kmean = km3[:, :tm // MOBA_BLOCK].reshape(S // MOBA_BLOCK, AW)
    return kp, vp, kbp, vbp, kmean, ks, vs


def _select_bias(gate, row_block, nb):
    blk = lax.broadcasted_iota(jnp.int32, gate.shape, 1).astype(_F32)
    g = jnp.where(blk < row_block, gate, -jnp.inf)
    sel = jnp.zeros(gate.shape, jnp.bool_)
    for _ in range(MOBA_TOPK):
        mx = jnp.max(g, axis=1, keepdims=True)
        is_max = jnp.logical_and(g == mx, g > -jnp.inf)
        first = jnp.min(jnp.where(is_max, blk, float(nb)), axis=1, keepdims=True)
        pick = blk == first
        sel = jnp.logical_or(sel, pick)
        g = jnp.where(pick, -jnp.inf, g)
    return jnp.where(sel, 0.0, MASK_BIAS)


def _q_kernel(xp_ref, xs_ref, wq_ref, inv_ref, kmt_ref, qbp_ref, bias_ref, qs_ref,
              xb_ref, cos_ref, sin_ref, *, npt, ns, past_len):
    i, j = pl.program_id(0), pl.program_id(1)
    tn = wq_ref.shape[1]
    nb = kmt_ref.shape[1]

    def body(x_ref, xb, cos_r, sin_r, pos_fn, prompt):
        rows = x_ref.shape[0]

        @pl.when(j == 0)
        def _():
            xb[...] = x_ref[...].astype(_BF16)
            _rope_tables(pos_fn(rows), inv_ref, cos_r, sin_r)

        q = jnp.dot(xb[...], wq_ref[...], preferred_element_type=_F32)
        cos, sin = cos_r[...], sin_r[...]
        heads = [_rope_head(q[:, h * LANES:(h + 1) * LANES], cos, sin)
                 for h in range(tn // LANES)]
        q = jnp.concatenate(heads, axis=1)
        if not prompt:
            qs_ref[...] = q
            return
        qbp_ref[...] = q.astype(_BF16)
        row_block = (i * rows + lax.broadcasted_iota(jnp.int32, (rows, 1), 0)) // MOBA_BLOCK
        row_block = row_block.astype(_F32)
        biases = []
        for h, qh in enumerate(heads):
            gate = jnp.dot(qh, kmt_ref[h * LANES:(h + 1) * LANES, :],
                           preferred_element_type=_F32, precision=lax.Precision.HIGHEST)
            biases.append(_select_bias(gate, row_block, nb))
        bias_ref[...] = jnp.concatenate(biases, axis=1)

    _dual(i, npt,
          lambda: body(xp_ref, xb_ref, cos_ref, sin_ref, lambda rows: _row_positions(i, rows),
                       True),
          lambda: body(xs_ref, xb_ref.at[pl.ds(0, ns)], cos_ref.at[pl.ds(0, ns)],
                       sin_ref.at[pl.ds(0, ns)],
                       lambda rows: jnp.full((rows, LANES), float(past_len), _F32), False))


def _q_col_tile(attn_width):
    return _largest_tile(attn_width, (512, 256, 128))


def _q_proj(xp, xs, wq, inv128, kmean_t, past_len, tm):
    S, D = xp.shape
    ns = xs.shape[0]
    AW = wq.shape[-1]
    nb = kmean_t.shape[1]
    tn = _q_col_tile(AW)
    hpb = tn // LANES
    npt, nj = S // tm, AW // tn
    assert nj == 1 or (hpb * nb) % LANES == 0
    prow = _prompt_tile(npt, nj)
    return pl.pallas_call(
        functools.partial(_q_kernel, npt=npt, ns=ns, past_len=past_len),
        grid=(npt + 1, nj),
        in_specs=[pl.BlockSpec((tm, D), _prompt_rows(npt)),
                  pl.BlockSpec((ns, D), lambda i, j: (0, 0)),
                  pl.BlockSpec((None, D, tn), lambda i, j: (0, 0, j)),
                  pl.BlockSpec((1, LANES), lambda i, j: (0, 0)),
                  pl.BlockSpec((tn, nb), lambda i, j: (j, 0))],
        out_specs=[pl.BlockSpec((tm, tn), prow),
                   pl.BlockSpec((tm, hpb * nb), prow),
                   pl.BlockSpec((ns, tn), _sample_tile(npt))],
        out_shape=[jax.ShapeDtypeStruct((S, AW), _BF16),
                   jax.ShapeDtypeStruct((S, (AW // LANES) * nb), _F32),
                   jax.ShapeDtypeStruct((ns, AW), _F32)],
        scratch_shapes=[pltpu.VMEM((tm, D), _BF16), pltpu.VMEM((tm, LANES), _F32),
                        pltpu.VMEM((tm, LANES), _F32)],
        compiler_params=_params(2),
        name="q_rope_select",
    )(xp, xs, wq, inv128, kmean_t)


def _attn_prompt_kernel(q_ref, k_ref, v_ref, bias_ref, o_ref, m_ref, l_ref, acc_ref,
                        *, hpb, nb, scale_log2e):
    h, qi = pl.program_id(0), pl.program_id(1)
    tq = q_ref.shape[0]
    q = q_ref[...]

    j0 = pl.multiple_of(qi * MOBA_BLOCK, MOBA_BLOCK)
    kd = k_ref[pl.ds(j0, MOBA_BLOCK), :]
    vd = v_ref[pl.ds(j0, MOBA_BLOCK), :]
    s = lax.dot_general(q, kd, _NT, preferred_element_type=_F32) * scale_log2e
    row = lax.broadcasted_iota(jnp.int32, s.shape, 0)
    col = lax.broadcasted_iota(jnp.int32, s.shape, 1)
    s = jnp.where(col <= row, s, -jnp.inf)
    m = jnp.max(s, axis=1, keepdims=True)
    p = jnp.exp2(s - m)
    m_ref[...] = m
    l_ref[...] = jnp.sum(p, axis=1, keepdims=True)
    acc_ref[...] = jnp.dot(p.astype(_BF16), vd, preferred_element_type=_F32)

    bias = bias_ref[...].astype(_BF16)
    if bias.shape[1] < LANES:
        bias = jnp.concatenate(
            [bias, jnp.zeros((tq, LANES - bias.shape[1]), _BF16)], axis=1)
    q_aug = jnp.concatenate([q, bias], axis=1)
    lane = lax.broadcasted_iota(jnp.int32, (MOBA_BLOCK, LANES), 1)
    lane_base = (h % hpb) * nb

    def step(j, carry):
        k0 = pl.multiple_of(j * MOBA_BLOCK, MOBA_BLOCK)
        kj = k_ref[pl.ds(k0, MOBA_BLOCK), :]
        vj = v_ref[pl.ds(k0, MOBA_BLOCK), :]
        onehot = jnp.where(lane == lane_base + j, 1.0, 0.0).astype(_BF16)
        k_aug = jnp.concatenate([kj, onehot], axis=1)
        sj = lax.dot_general(q_aug, k_aug, _NT, preferred_element_type=_F32) * scale_log2e
        m_old = m_ref[...]
        m_new = jnp.maximum(m_old, jnp.max(sj, axis=1, keepdims=True))
        corr = jnp.exp2(m_old - m_new)
        pj = jnp.exp2(sj - m_new)
        l_ref[...] = corr * l_ref[...] + jnp.sum(pj, axis=1, keepdims=True)
        acc_ref[...] = corr * acc_ref[...] + jnp.dot(pj.astype(_BF16), vj,
                                                     preferred_element_type=_F32)
        m_ref[...] = m_new
        return carry

    lax.fori_loop(0, qi, step, 0)
    o_ref[...] = (acc_ref[...] / l_ref[...]).astype(o_ref.dtype)


def _attn_prompt(qb, kb, vb, bias, head_dim, hpb, nb):
    S, AW = qb.shape
    assert head_dim == LANES
    H = AW // head_dim
    tq = MOBA_BLOCK
    return pl.pallas_call(
        functools.partial(_attn_prompt_kernel, hpb=hpb, nb=nb,
                          scale_log2e=head_dim ** -0.5 * math.log2(math.e)),
        grid=(H, S // tq),
        in_specs=[pl.BlockSpec((tq, head_dim), lambda h, i: (i, h)),
                  pl.BlockSpec((S, head_dim), lambda h, i: (0, h)),
                  pl.BlockSpec((S, head_dim), lambda h, i: (0, h)),
                  pl.BlockSpec((tq, hpb * nb), lambda h, i: (i, h // hpb))],
        out_specs=pl.BlockSpec((tq, head_dim), lambda h, i: (i, h)),
        out_shape=jax.ShapeDtypeStruct((S, AW), _BF16),
        scratch_shapes=[pltpu.VMEM((tq, 1), _F32), pltpu.VMEM((tq, 1), _F32),
                        pltpu.VMEM((tq, head_dim), _F32)],
        compiler_params=_params(2),
        name="moba_prompt",
    )(qb, kb, vb, bias)


def _attn_sample_kernel(pt_ref, q_ref, kn_ref, vn_ref, k0_ref, k1_ref, v0_ref, v1_ref, o_ref,
                        m_ref, l_ref, g_ref, acc_ref, *, n_blocks, scale_log2e):
    del pt_ref
    n = pl.program_id(1)
    page, H, Dh = k0_ref.shape
    rows = page * H
    q = q_ref[...]
    q_hi = q.astype(_BF16)
    q_lo = (q - q_hi.astype(_F32)).astype(_BF16)
    q2 = jnp.concatenate([q_hi, q_lo], axis=0)

    def scores(k_ref):
        k16 = k_ref[...].reshape(rows, Dh).astype(_BF16)
        s2 = lax.dot_general(q2, k16, _NT, preferred_element_type=_F32)
        return s2[0:H] + s2[H:2 * H]

    s = jnp.concatenate([scores(k0_ref), scores(k1_ref)], axis=1)
    head_of_row = lax.broadcasted_iota(jnp.int32, s.shape, 1) % H
    own = head_of_row == lax.broadcasted_iota(jnp.int32, s.shape, 0)
    sm = jnp.where(own, s, -jnp.inf)
    m = jnp.max(sm, axis=1, keepdims=True)
    p = jnp.exp2((sm - m) * scale_log2e)
    l = jnp.sum(p, axis=1, keepdims=True)
    g = jnp.sum(jnp.where(own, s, 0.0), axis=1, keepdims=True)
    p16 = p.astype(_BF16)
    o = (jnp.dot(p16[:, 0:rows], v0_ref[...].reshape(rows, Dh).astype(_BF16),
                 preferred_element_type=_F32)
         + jnp.dot(p16[:, rows:2 * rows], v1_ref[...].reshape(rows, Dh).astype(_BF16),
                   preferred_element_type=_F32))
    m_ref[n] = jnp.broadcast_to(m, (H, LANES))
    l_ref[n] = jnp.broadcast_to(l, (H, LANES))
    g_ref[n] = jnp.broadcast_to(g, (H, LANES))
    acc_ref[n] = o

    @pl.when(n == n_blocks - 1)
    def _():
        gates = [g_ref[b] for b in range(n_blocks)]
        s_own = jnp.sum(q * kn_ref[...], axis=1, keepdims=True)
        m_tot = jnp.broadcast_to(s_own, (H, LANES))
        sel = []
        for b in range(n_blocks):
            rank = jnp.zeros((H, LANES), _F32)
            for c in range(n_blocks):
                if c == b:
                    continue
                ahead = gates[c] >= gates[b] if c < b else gates[c] > gates[b]
                rank = rank + jnp.where(ahead, 1.0, 0.0)
            sel.append(rank < float(MOBA_TOPK))
            m_tot = jnp.where(sel[b], jnp.maximum(m_tot, m_ref[b]), m_tot)
        w_own = jnp.exp2((s_own - m_tot) * scale_log2e)
        den = w_own
        num = w_own * vn_ref[...]
        for b in range(n_blocks):
            w = jnp.where(sel[b], jnp.exp2((m_ref[b] - m_tot) * scale_log2e), 0.0)
            den = den + w * l_ref[b]
            num = num + w * acc_ref[b]
        o_ref[...] = num / den


def _attn_sample(q3, kn3, vn3, cache_k, cache_v, page_table, past_len):
    Bd, H, Dh = q3.shape
    page = cache_k.shape[1]
    assert Dh == LANES and MOBA_BLOCK == 2 * page
    assert past_len % MOBA_BLOCK == 0
    n_blocks = past_len // MOBA_BLOCK
    assert n_blocks >= MOBA_TOPK
    n_pages = page_table.shape[1]
    pt = page_table.reshape(-1)
    tok = pl.BlockSpec((None, H, Dh), lambda b, n, pt: (b, 0, 0))

    def page_spec(which):
        return pl.BlockSpec((None, page, H, Dh),
                            lambda b, n, pt: (pt[b * n_pages + 2 * n + which], 0, 0, 0))

    grid_spec = pltpu.PrefetchScalarGridSpec(
        num_scalar_prefetch=1,
        grid=(Bd, n_blocks),
        in_specs=[tok, tok, tok, page_spec(0), page_spec(1), page_spec(0), page_spec(1)],
        out_specs=tok,
        scratch_shapes=[pltpu.VMEM((n_blocks, H, LANES), _F32),
                        pltpu.VMEM((n_blocks, H, LANES), _F32),
                        pltpu.VMEM((n_blocks, H, LANES), _F32),
                        pltpu.VMEM((n_blocks, H, Dh), _F32)],
    )
    return pl.pallas_call(
        functools.partial(_attn_sample_kernel, n_blocks=n_blocks,
                          scale_log2e=Dh ** -0.5 * math.log2(math.e)),
        grid_spec=grid_spec,
        out_shape=jax.ShapeDtypeStruct((Bd, H, Dh), _F32),
        compiler_params=_params(2),
        name="moba_sample",
    )(pt, q3, kn3, vn3, cache_k, cache_k, cache_v, cache_v)


def _oproj_kernel(ap_ref, as_ref, xp_ref, xs_ref, wo_ref, g_ref, b_ref, op_ref, os_ref,
                  *, npt, alpha):
    i = pl.program_id(0)

    def body(a_ref, x_ref, o_ref):
        m = jnp.dot(a_ref[...].astype(_BF16), wo_ref[...], preferred_element_type=_F32)
        o_ref[...] = _layer_norm(alpha * x_ref[...] + m, g_ref[...], b_ref[...])

    _dual(i, npt, lambda: body(ap_ref, xp_ref, op_ref), lambda: body(as_ref, xs_ref, os_ref))


def _oproj(ap, a_s, xp, xs, wo, g4, b4, layer, alpha, tm):
    S, D = xp.shape
    ns = xs.shape[0]
    AW = ap.shape[1]
    npt = S // tm
    vec = pl.BlockSpec((None, None, 1, D), lambda i: (layer, 1, 0, 0))
    return pl.pallas_call(
        functools.partial(_oproj_kernel, npt=npt, alpha=alpha),
        grid=(npt + 1,),
        in_specs=[pl.BlockSpec((tm, AW), _prompt_rows(npt)),
                  pl.BlockSpec((ns, AW), lambda i: (0, 0)),
                  pl.BlockSpec((tm, D), _prompt_rows(npt)),
                  pl.BlockSpec((ns, D), lambda i: (0, 0)),
                  pl.BlockSpec((None, AW, D), lambda i: (0, 0, 0)), vec, vec],
        out_specs=[pl.BlockSpec((tm, D), _prompt_rows(npt)),
                   pl.BlockSpec((ns, D), lambda i: (0, 0))],
        out_shape=[jax.ShapeDtypeStruct((S, D), _F32), jax.ShapeDtypeStruct((ns, D), _F32)],
        compiler_params=_params(1),
        name="oproj_postnorm",
    )(ap, a_s, xp, xs, wo, g4, b4)


def kernel(x_prompt, x_sample, state_conv, cache_k, cache_v, page_table, ffn_w_gate, ffn_w_up, ffn_w_down, ln_g, ln_b, conv_w_pw1, conv_b_pw1, conv_w_dw, conv_b_dw, conv_ln_g, conv_ln_b, conv_w_pw2, conv_b_pw2, attn_w_q, attn_w_o, w_kv):
    B, S, D = x_prompt.shape
    Bd, Sd, _ = x_sample.shape
    depth = ffn_w_gate.shape[0]
    n_phys, page, H, Dh = cache_k.shape
    past_len = page_table.shape[1] * page
    conv_w = conv_w_dw.shape[1]
    assert B == 1 and Sd == 1 and depth == 2 and conv_w_pw1.shape[0] == 1
    assert S % MOBA_BLOCK == 0 and H * Dh == attn_w_q.shape[-1]
    alpha = (2.0 * depth) ** 0.25
    tm = _largest_tile(S, (512, 256))

    wg, wu, wd = (w.astype(_BF16) for w in (ffn_w_gate, ffn_w_up, ffn_w_down))
    g4, b4 = ln_g[:, :, None, :], ln_b[:, :, None, :]
    half = Dh // 2
    inv = ROPE_THETA ** (-jnp.arange(half, dtype=_F32) / half)
    inv128 = jnp.concatenate([inv, inv])[None, :]

    xp, xs = x_prompt.reshape(S, D), x_sample.reshape(Bd, D)

    xp, xs = _ffn(xp, xs, wg, wu, wd, g4, b4, 0, 0, alpha, tm)
    up, us = _glu(xp, xs, conv_w_pw1.astype(_BF16), conv_b_pw1, tm)
    xp, xs = _conv_mixer(up, us, state_conv[0], xp, xs, conv_w_dw[0], conv_b_dw, conv_ln_g,
                         conv_ln_b, conv_w_pw2[0].astype(_BF16), conv_b_pw2, ln_g[0, 1:2],
                         ln_b[0, 1:2], alpha)
    xp, xs = _ffn(xp, xs, wg, wu, wd, g4, b4, 0, 1, alpha, tm)
    kp, vp, kbp, vbp, kmean, ks, vs = _kv(xp, xs, w_kv.astype(_BF16), inv128, past_len, tm)

    xp, xs = _ffn(xp, xs, wg, wu, wd, g4, b4, 1, 0, alpha, tm)
    nb = S // MOBA_BLOCK
    qbp, bias, qs = _q_proj(xp, xs, attn_w_q.astype(_BF16), inv128, kmean.T, past_len, tm)
    hpb = _q_col_tile(H * Dh) // LANES
    ap = _attn_prompt(qbp, kbp, vbp, bias, Dh, hpb, nb)
    a_s = _attn_sample(qs.reshape(Bd, H, Dh), ks.reshape(Bd, H, Dh), vs.reshape(Bd, H, Dh),
                       cache_k, cache_v, page_table, past_len)
    xp, xs = _oproj(ap, a_s.reshape(Bd, H * Dh), xp, xs, attn_w_o.astype(_BF16), g4, b4, 1,
                    alpha, tm)
    xp, xs = _ffn(xp, xs, wg, wu, wd, g4, b4, 1, 1, alpha, tm)

    conv_p = up[S - (conv_w - 1):].reshape(1, 1, conv_w - 1, D)
    conv_s = jnp.concatenate([state_conv[:, :, 1:], us[None, :, None, :]], axis=2)
    return (xp.reshape(1, S, D), xs.reshape(Bd, 1, D), conv_p, conv_s,
            kp.reshape(1, S, H, Dh), vp.reshape(1, S, H, Dh),
            ks.reshape(Bd, 1, H, Dh), vs.reshape(Bd, 1, H, Dh))
```

```python
import functools
import math

import jax
import jax.numpy as jnp
from jax import lax
from jax.experimental import pallas as pl
from jax.experimental.pallas import tpu as pltpu

_F32 = jnp.float32
_BF16 = jnp.bfloat16

MOBA_BLOCK = 256
MOBA_TOPK = 3
ROPE_THETA = 10000.0
LN_EPS = 1e-5
LANES = 128
MASK_BIAS = -1e30
V7X_VMEM_LIMIT = 56 * 1024 * 1024

_NT = (((1,), (1,)), ((), ()))


def _params(n_grid_dims, vmem_bytes=V7X_VMEM_LIMIT):
    return pltpu.CompilerParams(
        dimension_semantics=("arbitrary",) * n_grid_dims, vmem_limit_bytes=vmem_bytes)


def _largest_tile(n, candidates):
    for c in candidates:
        if n % c == 0:
            return c
    return n


def _layer_norm(z, g, b):
    mu = jnp.mean(z, axis=-1, keepdims=True)
    zc = z - mu
    var = jnp.mean(zc * zc, axis=-1, keepdims=True)
    return zc * lax.rsqrt(var + LN_EPS) * g + b


def _sigmoid(a):
    return 1.0 / (1.0 + jnp.exp(-a))


def _dual(i, n_prompt_tiles, prompt_fn, sample_fn):
    pl.when(i < n_prompt_tiles)(prompt_fn)
    pl.when(i == n_prompt_tiles)(sample_fn)


def _prompt_rows(npt):
    return lambda i, *_: (jnp.minimum(i, npt - 1), 0)


def _prompt_tile(npt, nj):
    return lambda i, j: (jnp.minimum(i, npt - 1), jnp.where(i < npt, j, nj - 1))


def _sample_tile(npt):
    return lambda i, j: (0, jnp.where(i < npt, 0, j))


def _ffn_kernel(xp_ref, xs_ref, wg_ref, wu_ref, wd_ref, g_ref, b_ref, op_ref, os_ref, xb_ref,
                *, npt, ns, alpha):
    i, j, nj = pl.program_id(0), pl.program_id(1), pl.num_programs(1)

    def body(x_ref, o_ref, xb):
        @pl.when(j == 0)
        def _():
            xb[...] = x_ref[...].astype(_BF16)
            o_ref[...] = jnp.zeros_like(o_ref)

        x16 = xb[...]
        a = jnp.dot(x16, wg_ref[...], preferred_element_type=_F32)
        u = jnp.dot(x16, wu_ref[...], preferred_element_type=_F32)
        h = (a * _sigmoid(a)) * u
        o_ref[...] += jnp.dot(h.astype(_BF16), wd_ref[...], preferred_element_type=_F32)

        @pl.when(j == nj - 1)
        def _():
            z = alpha * x_ref[...] + 0.5 * o_ref[...]
            o_ref[...] = _layer_norm(z, g_ref[...], b_ref[...])

    _dual(i, npt, lambda: body(xp_ref, op_ref, xb_ref),
          lambda: body(xs_ref, os_ref, xb_ref.at[pl.ds(0, ns)]))


def _ffn(xp, xs, wg, wu, wd, g4, b4, layer, which, alpha, tm):
    norm = 2 * which
    S, D = xp.shape
    ns = xs.shape[0]
    F = wg.shape[-1]
    tf = _largest_tile(F, (512, 256, 128))
    npt = S // tm
    w_in = pl.BlockSpec((None, None, D, tf), lambda i, j: (layer, which, 0, j))
    w_out = pl.BlockSpec((None, None, tf, D), lambda i, j: (layer, which, j, 0))
    vec = pl.BlockSpec((None, None, 1, D), lambda i, j: (layer, norm, 0, 0))
    single = pl.Buffered(1)
    return pl.pallas_call(
        functools.partial(_ffn_kernel, npt=npt, ns=ns, alpha=alpha),
        grid=(npt + 1, F // tf),
        in_specs=[pl.BlockSpec((tm, D), _prompt_rows(npt), pipeline_mode=single),
                  pl.BlockSpec((ns, D), lambda i, j: (0, 0), pipeline_mode=single),
                  w_in, w_in, w_out, vec, vec],
        out_specs=[pl.BlockSpec((tm, D), _prompt_rows(npt), pipeline_mode=single),
                   pl.BlockSpec((ns, D), lambda i, j: (0, 0), pipeline_mode=single)],
        out_shape=[jax.ShapeDtypeStruct((S, D), _F32), jax.ShapeDtypeStruct((ns, D), _F32)],
        scratch_shapes=[pltpu.VMEM((tm, D), _BF16)],
        compiler_params=_params(2),
        name="ffn_postnorm",
    )(xp, xs, wg, wu, wd, g4, b4)


def _glu_kernel(xp_ref, xs_ref, wa_ref, wb_ref, ba_ref, bb_ref, op_ref, os_ref, xb_ref,
                *, npt, ns):
    i, j = pl.program_id(0), pl.program_id(1)

    def body(x_ref, o_ref, xb):
        @pl.when(j == 0)
        def _():
            xb[...] = x_ref[...].astype(_BF16)

        x16 = xb[...]
        a = jnp.dot(x16, wa_ref[...], preferred_element_type=_F32) + ba_ref[...]
        gate = jnp.dot(x16, wb_ref[...], preferred_element_type=_F32) + bb_ref[...]
        o_ref[...] = a * _sigmoid(gate)

    _dual(i, npt, lambda: body(xp_ref, op_ref, xb_ref),
          lambda: body(xs_ref, os_ref, xb_ref.at[pl.ds(0, ns)]))


def _glu(xp, xs, w_pw1, b_pw1, tm):
    S, D = xp.shape
    ns = xs.shape[0]
    tn = _largest_tile(D, (512, 256, 128))
    npt, nj = S // tm, D // tn
    b3 = b_pw1.reshape(1, 1, 2 * D)
    return pl.pallas_call(
        functools.partial(_glu_kernel, npt=npt, ns=ns),
        grid=(npt + 1, nj),
        in_specs=[pl.BlockSpec((tm, D), _prompt_rows(npt)),
                  pl.BlockSpec((ns, D), lambda i, j: (0, 0)),
                  pl.BlockSpec((None, D, tn), lambda i, j: (0, 0, j)),
                  pl.BlockSpec((None, D, tn), lambda i, j: (0, 0, j + nj)),
                  pl.BlockSpec((None, 1, tn), lambda i, j: (0, 0, j)),
                  pl.BlockSpec((None, 1, tn), lambda i, j: (0, 0, j + nj))],
        out_specs=[pl.BlockSpec((tm, tn), _prompt_tile(npt, nj)),
                   pl.BlockSpec((ns, tn), _sample_tile(npt))],
        out_shape=[jax.ShapeDtypeStruct((S, D), _F32), jax.ShapeDtypeStruct((ns, D), _F32)],
        scratch_shapes=[pltpu.VMEM((tm, D), _BF16)],
        compiler_params=_params(2),
        name="pw1_glu",
    )(xp, xs, w_pw1, w_pw1, b3, b3)


def _conv_tail(y, x1, gcn_ref, bcn_ref, w2_ref, b2_ref, g_ref, b_ref, alpha):
    z = _layer_norm(y, gcn_ref[...], bcn_ref[...])
    z = z * _sigmoid(z)
    m = jnp.dot(z.astype(_BF16), w2_ref[...], preferred_element_type=_F32) + b2_ref[...]
    return _layer_norm(alpha * x1 + m, g_ref[...], b_ref[...])


CONV_HALO = 32
CONV_ROWS = 64
CONV_COLS = 256
SUBLANES = 8


def _conv_prompt_kernel(u_ref, halo_ref, x1_ref, wdw_ref, bdw_ref, gcn_ref, bcn_ref, w2_ref,
                        b2_ref, g_ref, b_ref, o_ref, ext_ref, y_ref, wb_ref, *, conv_w, alpha):
    i = pl.program_id(0)
    tm, D = u_ref.shape

    @pl.when(i == 0)
    def _():
        for k in range(conv_w):
            wb_ref[k] = jnp.broadcast_to(wdw_ref[k:k + 1, :], (SUBLANES, D))

    ext_ref[0:CONV_HALO, :] = jnp.where(i == 0, 0.0, halo_ref[...])
    ext_ref[CONV_HALO:, :] = u_ref[...]
    lead = CONV_HALO - (conv_w - 1)
    cc = min(CONV_COLS, D)

    def row_chunk(r, carry):
        r0 = pl.multiple_of(r * CONV_ROWS, CONV_ROWS)
        for c in range(D // cc):
            cols = slice(c * cc, (c + 1) * cc)
            base = ext_ref[pl.ds(r0, CONV_ROWS + CONV_HALO), cols]
            acc = jnp.zeros((CONV_ROWS // SUBLANES, SUBLANES, cc), _F32)
            for shift in range(SUBLANES):
                offs = [o for o in range(shift, lead + conv_w, SUBLANES) if o >= lead]
                if not offs:
                    continue
                win = base[offs[0]:offs[-1] + CONV_ROWS]
                for o in offs:
                    tap = win[o - offs[0]:o - offs[0] + CONV_ROWS]
                    tap = tap.reshape(CONV_ROWS // SUBLANES, SUBLANES, cc)
                    acc = acc + wb_ref[o - lead, :, cols][None] * tap
            y_ref[pl.ds(r0, CONV_ROWS), cols] = acc.reshape(CONV_ROWS, cc) + bdw_ref[:, cols]
        return carry

    lax.fori_loop(0, tm // CONV_ROWS, row_chunk, 0)
    o_ref[...] = _conv_tail(y_ref[...], x1_ref[...], gcn_ref, bcn_ref, w2_ref, b2_ref,
                            g_ref, b_ref, alpha)


def _conv_sample_kernel(st_ref, u_ref, x1_ref, wdw_ref, bdw_ref, gcn_ref, bcn_ref, w2_ref,
                        b2_ref, g_ref, b_ref, o_ref, *, conv_w, alpha):
    w = wdw_ref[...]
    y = jnp.sum(st_ref[...] * w[None, 0:conv_w - 1, :], axis=1)
    y = y + u_ref[...] * w[conv_w - 1:conv_w, :] + bdw_ref[...]
    o_ref[...] = _conv_tail(y, x1_ref[...], gcn_ref, bcn_ref, w2_ref, b2_ref, g_ref, b_ref,
                            alpha)


def _conv_mixer(up, us, state, x1p, x1s, wdw, bdw, gcn, bcn, w2, b2, g, b, alpha):
    S, D = up.shape
    ns = us.shape[0]
    conv_w = wdw.shape[0]
    assert conv_w - 1 <= CONV_HALO
    tm = _largest_tile(S, (256, 128))
    per_halo = tm // CONV_HALO
    full = lambda shape: pl.BlockSpec(shape, lambda i: (0,) * len(shape))
    shared = [full((conv_w, D)), full((1, D)), full((1, D)), full((1, D)), full((D, D)),
              full((1, D)), full((1, D)), full((1, D))]
    shared_args = (wdw, bdw, gcn, bcn, w2, b2, g, b)
    xp = pl.pallas_call(
        functools.partial(_conv_prompt_kernel, conv_w=conv_w, alpha=alpha),
        grid=(S // tm,),
        in_specs=[pl.BlockSpec((tm, D), lambda i: (i, 0)),
                  pl.BlockSpec((CONV_HALO, D), lambda i: (jnp.maximum(i * per_halo - 1, 0), 0)),
                  pl.BlockSpec((tm, D), lambda i: (i, 0))] + shared,
        out_specs=pl.BlockSpec((tm, D), lambda i: (i, 0)),
        out_shape=jax.ShapeDtypeStruct((S, D), _F32),
        scratch_shapes=[pltpu.VMEM((CONV_HALO + tm, D), _F32), pltpu.VMEM((tm, D), _F32),
                        pltpu.VMEM((conv_w, SUBLANES, D), _F32)],
        compiler_params=_params(1),
        name="conv_prompt",
    )(up, up, x1p, *shared_args)
    tb = _largest_tile(ns, (16, 8))
    xs = pl.pallas_call(
        functools.partial(_conv_sample_kernel, conv_w=conv_w, alpha=alpha),
        grid=(ns // tb,),
        in_specs=[pl.BlockSpec((tb, conv_w - 1, D), lambda i: (i, 0, 0)),
                  pl.BlockSpec((tb, D), lambda i: (i, 0)),
                  pl.BlockSpec((tb, D), lambda i: (i, 0))] + shared,
        out_specs=pl.BlockSpec((tb, D), lambda i: (i, 0)),
        out_shape=jax.ShapeDtypeStruct((ns, D), _F32),
        compiler_params=_params(1),
        name="conv_sample",
    )(state, us, x1s, *shared_args)
    return xp, xs


def _rope_tables(pos, inv_ref, cos_ref, sin_ref):
    ang = pos * inv_ref[...]
    lane = lax.broadcasted_iota(jnp.int32, ang.shape, 1)
    cos_ref[...] = jnp.cos(ang)
    sin_ref[...] = jnp.where(lane < LANES // 2, -1.0, 1.0) * jnp.sin(ang)


def _rope_head(x, cos, sin):
    return x * cos + pltpu.roll(x, LANES // 2, axis=1) * sin


def _row_positions(i, tm):
    return (i * tm + lax.broadcasted_iota(jnp.int32, (tm, LANES), 0)).astype(_F32)


def _kv_kernel(xp_ref, xs_ref, wk_ref, wv_ref, inv_ref,
               kp_ref, vp_ref, kbp_ref, vbp_ref, km_ref, ks_ref, vs_ref,
               xb_ref, cos_ref, sin_ref, *, npt, ns, past_len):
    i, j = pl.program_id(0), pl.program_id(1)
    tn = wk_ref.shape[1]

    def body(x_ref, xb, cos_r, sin_r, pos_fn, k_ref, v_ref, kb_ref, vb_ref, kmean_ref):
        rows = x_ref.shape[0]

        @pl.when(j == 0)
        def _():
            xb[...] = x_ref[...].astype(_BF16)
            _rope_tables(pos_fn(rows), inv_ref, cos_r, sin_r)

        x16 = xb[...]
        k = jnp.dot(x16, wk_ref[...], preferred_element_type=_F32)
        v = jnp.dot(x16, wv_ref[...], preferred_element_type=_F32)
        cos, sin = cos_r[...], sin_r[...]
        k = jnp.concatenate(
            [_rope_head(k[:, h * LANES:(h + 1) * LANES], cos, sin) for h in range(tn // LANES)],
            axis=1)
        k_ref[...] = k
        v_ref[...] = v
        if kb_ref is not None:
            kb_ref[...] = k.astype(_BF16)
            nblk = rows // MOBA_BLOCK
            for b in range(nblk):
                vb_ref[b] = v[b * MOBA_BLOCK:(b + 1) * MOBA_BLOCK].T.astype(_BF16)
            means = [jnp.mean(k[b * MOBA_BLOCK:(b + 1) * MOBA_BLOCK], axis=0, keepdims=True)
                     for b in range(nblk)]
            means.append(jnp.zeros((kmean_ref.shape[0] - nblk, tn), _F32))
            kmean_ref[...] = jnp.concatenate(means, axis=0)

    _dual(i, npt,
          lambda: body(xp_ref, xb_ref, cos_ref, sin_ref, lambda rows: _row_positions(i, rows),
                       kp_ref, vp_ref, kbp_ref, vbp_ref, km_ref),
          lambda: body(xs_ref, xb_ref.at[pl.ds(0, ns)], cos_ref.at[pl.ds(0, ns)],
                       sin_ref.at[pl.ds(0, ns)],
                       lambda rows: jnp.full((rows, LANES), float(past_len), _F32),
                       ks_ref, vs_ref, None, None, None))


KMEAN_ROWS = 8


def _kv(xp, xs, w_kv, inv128, past_len, tm):
    S, D = xp.shape
    ns = xs.shape[0]
    AW = w_kv.shape[1] // 2
    tn = _largest_tile(AW, (512, 256, 128))
    npt, nj = S // tm, AW // tn
    assert tm % MOBA_BLOCK == 0 and tm // MOBA_BLOCK <= KMEAN_ROWS
    prow, srow = _prompt_tile(npt, nj), _sample_tile(npt)
    outs = pl.pallas_call(
        functools.partial(_kv_kernel, npt=npt, ns=ns, past_len=past_len),
        grid=(npt + 1, nj),
        in_specs=[pl.BlockSpec((tm, D), _prompt_rows(npt)),
                  pl.BlockSpec((ns, D), lambda i, j: (0, 0)),
                  pl.BlockSpec((D, tn), lambda i, j: (0, j)),
                  pl.BlockSpec((D, tn), lambda i, j: (0, j + nj)),
                  pl.BlockSpec((1, LANES), lambda i, j: (0, 0))],
        out_specs=[pl.BlockSpec((tm, tn), prow), pl.BlockSpec((tm, tn), prow),
                   pl.BlockSpec((tm, tn), prow),
                   pl.BlockSpec((tm // MOBA_BLOCK, tn, MOBA_BLOCK),
                                lambda i, j: (prow(i, j)[0], prow(i, j)[1], 0)),
                   pl.BlockSpec((None, KMEAN_ROWS, tn),
                                lambda i, j: (prow(i, j)[0], 0, prow(i, j)[1])),
                   pl.BlockSpec((ns, tn), srow), pl.BlockSpec((ns, tn), srow)],
        out_shape=[jax.ShapeDtypeStruct((S, AW), _F32), jax.ShapeDtypeStruct((S, AW), _F32),
                   jax.ShapeDtypeStruct((S, AW), _BF16),
                   jax.ShapeDtypeStruct((S // MOBA_BLOCK, AW, MOBA_BLOCK), _BF16),
                   jax.ShapeDtypeStruct((npt, KMEAN_ROWS, AW), _F32),
                   jax.ShapeDtypeStruct((ns, AW), _F32), jax.ShapeDtypeStruct((ns, AW), _F32)],
        scratch_shapes=[pltpu.VMEM((tm, D), _BF16), pltpu.VMEM((tm, LANES), _F32),
                        pltpu.VMEM((tm, LANES), _F32)],
        compiler_params=_params(2),
        name="kv_rope",
    )(xp, xs, w_kv, w_kv, inv128)
    kp, vp, kbp, vtp, km3, ks, vs = outs
    kmean = km3[:, :tm // MOBA_BLOCK].reshape(S // MOBA_BLOCK, AW)
    return kp, vp, kbp, vtp, kmean, ks, vs


def _select_bias(gate_t, query_block):
    nb = gate_t.shape[0]
    blk = lax.broadcasted_iota(jnp.int32, gate_t.shape, 0).astype(_F32)
    g = jnp.where(blk < query_block, gate_t, -jnp.inf)
    sel = jnp.zeros(gate_t.shape, jnp.bool_)
    for _ in range(MOBA_TOPK):
        mx = jnp.max(g, axis=0, keepdims=True)
        is_max = jnp.logical_and(g == mx, g > -jnp.inf)
        first = jnp.min(jnp.where(is_max, blk, float(nb)), axis=0, keepdims=True)
        pick = blk == first
        sel = jnp.logical_or(sel, pick)
        g = jnp.where(pick, -jnp.inf, g)
    return jnp.where(sel, 0.0, MASK_BIAS)


def _q_kernel(xp_ref, xs_ref, wq_ref, inv_ref, km_ref, qt_ref, bias_ref, qs_ref,
              xb_ref, cos_ref, sin_ref, *, npt, ns, past_len, score_scale):
    i, j = pl.program_id(0), pl.program_id(1)
    tn = wq_ref.shape[1]
    nb = km_ref.shape[0]

    def body(x_ref, xb, cos_r, sin_r, pos_fn, prompt):
        rows = x_ref.shape[0]

        @pl.when(j == 0)
        def _():
            xb[...] = x_ref[...].astype(_BF16)
            _rope_tables(pos_fn(rows), inv_ref, cos_r, sin_r)

        q = jnp.dot(xb[...], wq_ref[...], preferred_element_type=_F32)
        cos, sin = cos_r[...], sin_r[...]
        heads = [_rope_head(q[:, h * LANES:(h + 1) * LANES], cos, sin)
                 for h in range(tn // LANES)]
        q = jnp.concatenate(heads, axis=1)
        if not prompt:
            qs_ref[...] = q
            return
        qt_ref[...] = (q * score_scale).T.astype(_BF16)
        query_block = (i * rows + lax.broadcasted_iota(jnp.int32, (1, rows), 1)) // MOBA_BLOCK
        query_block = query_block.astype(_F32)
        for h, qh in enumerate(heads):
            gate_t = lax.dot_general(km_ref[:, h * LANES:(h + 1) * LANES], qh, _NT,
                                     preferred_element_type=_F32,
                                     precision=lax.Precision.HIGHEST)
            bias_ref[h * nb:(h + 1) * nb, :] = _select_bias(gate_t, query_block)

    _dual(i, npt,
          lambda: body(xp_ref, xb_ref, cos_ref, sin_ref, lambda rows: _row_positions(i, rows),
                       True),
          lambda: body(xs_ref, xb_ref.at[pl.ds(0, ns)], cos_ref.at[pl.ds(0, ns)],
                       sin_ref.at[pl.ds(0, ns)],
                       lambda rows: jnp.full((rows, LANES), float(past_len), _F32), False))


def _q_col_tile(attn_width):
    return _largest_tile(attn_width, (512, 256, 128))


def _q_proj(xp, xs, wq, inv128, kmean, past_len, tm):
    S, D = xp.shape
    ns = xs.shape[0]
    AW = wq.shape[-1]
    nb = kmean.shape[0]
    tn = _q_col_tile(AW)
    hpb = tn // LANES
    npt, nj = S // tm, AW // tn
    assert nj == 1 or (hpb * nb) % SUBLANES == 0
    prow = _prompt_tile(npt, nj)
    pcol = lambda i, j: prow(i, j)[::-1]
    return pl.pallas_call(
        functools.partial(_q_kernel, npt=npt, ns=ns, past_len=past_len,
                          score_scale=LANES ** -0.5 * math.log2(math.e)),
        grid=(npt + 1, nj),
        in_specs=[pl.BlockSpec((tm, D), _prompt_rows(npt)),
                  pl.BlockSpec((ns, D), lambda i, j: (0, 0)),
                  pl.BlockSpec((None, D, tn), lambda i, j: (0, 0, j)),
                  pl.BlockSpec((1, LANES), lambda i, j: (0, 0)),
                  pl.BlockSpec((nb, tn), lambda i, j: (0, j))],
        out_specs=[pl.BlockSpec((tn, tm), pcol),
                   pl.BlockSpec((hpb * nb, tm), pcol),
                   pl.BlockSpec((ns, tn), _sample_tile(npt))],
        out_shape=[jax.ShapeDtypeStruct((AW, S), _BF16),
                   jax.ShapeDtypeStruct(((AW // LANES) * nb, S), _F32),
                   jax.ShapeDtypeStruct((ns, AW), _F32)],
        scratch_shapes=[pltpu.VMEM((tm, D), _BF16), pltpu.VMEM((tm, LANES), _F32),
                        pltpu.VMEM((tm, LANES), _F32)],
        compiler_params=_params(2),
        name="q_rope_select",
    )(xp, xs, wq, inv128, kmean)


def _attn_prompt_kernel(qt_ref, k_ref, vt_ref, bias_ref, o_ref, qa_ref, sa_ref, sb_ref, m_ref,
                        l_ref, acc_ref, *, nb):
    qi = pl.program_id(1)
    tq = qt_ref.shape[1]
    n_heads = qt_ref.shape[0] // LANES

    bias = bias_ref[...].astype(_BF16)
    if bias.shape[0] < LANES:
        bias = jnp.concatenate(
            [bias, jnp.zeros((LANES - bias.shape[0], tq), _BF16)], axis=0)
    key_i = lax.broadcasted_iota(jnp.int32, (MOBA_BLOCK, tq), 0)
    qry_i = lax.broadcasted_iota(jnp.int32, (MOBA_BLOCK, tq), 1)
    j0 = pl.multiple_of(qi * MOBA_BLOCK, MOBA_BLOCK)
    head_cols = [slice(g * LANES, (g + 1) * LANES) for g in range(n_heads)]
    scores = [jnp.dot(k_ref[pl.ds(j0, MOBA_BLOCK), cols], qt_ref[cols, :],
                      preferred_element_type=_F32) for cols in head_cols]
    probs = []
    for g in range(n_heads):
        s = jnp.where(key_i <= qry_i, scores[g], -jnp.inf)
        m = jnp.max(s, axis=0, keepdims=True)
        p = jnp.exp2(s - m)
        m_ref[g] = m
        l_ref[g] = jnp.sum(p, axis=0, keepdims=True)
        probs.append(p.astype(_BF16))
    for g, cols in enumerate(head_cols):
        acc_ref[g] = jnp.dot(vt_ref[qi, cols, :], probs[g], preferred_element_type=_F32)
        qa_ref[g] = jnp.concatenate([qt_ref[cols, :], bias], axis=0)

    lane = lax.broadcasted_iota(jnp.int32, (MOBA_BLOCK, LANES), 1)

    def past_scores(j):
        k0 = pl.multiple_of(j * MOBA_BLOCK, MOBA_BLOCK)
        out = []
        for g, cols in enumerate(head_cols):
            onehot = jnp.where(lane == g * nb + j, 1.0, 0.0).astype(_BF16)
            k_aug = jnp.concatenate([k_ref[pl.ds(k0, MOBA_BLOCK), cols], onehot], axis=1)
            out.append(jnp.dot(k_aug, qa_ref[g], preferred_element_type=_F32))
        return out

    def compute_scores(j, s_ref):
        for g, s in enumerate(past_scores(j)):
            s_ref[g] = s

    def consume_scores(j, s_ref):
        probs, corrs = [], []
        for g in range(n_heads):
            s = s_ref[g]
            m_old = m_ref[g]
            m_new = jnp.maximum(m_old, jnp.max(s, axis=0, keepdims=True))
            corr = jnp.exp2(m_old - m_new)
            pj = jnp.exp2(s - m_new)
            l_ref[g] = corr * l_ref[g] + jnp.sum(pj, axis=0, keepdims=True)
            m_ref[g] = m_new
            probs.append(pj.astype(_BF16))
            corrs.append(corr)
        for g, cols in enumerate(head_cols):
            acc_ref[g] = corrs[g] * acc_ref[g] + jnp.dot(vt_ref[j, cols, :], probs[g],
                                                         preferred_element_type=_F32)

    last = nb - 1
    compute_scores(0, sa_ref)

    def step(t, carry):
        j = 2 * t
        compute_scores(jnp.minimum(j + 1, last), sb_ref)
        consume_scores(j, sa_ref)
        compute_scores(jnp.minimum(j + 2, last), sa_ref)
        consume_scores(jnp.minimum(j + 1, last), sb_ref)
        return carry

    lax.fori_loop(0, (qi + 1) // 2, step, 0)
    for g in range(n_heads):
        out_t = acc_ref[g] / l_ref[g]
        o_ref[:, g * LANES:(g + 1) * LANES] = out_t.T.astype(o_ref.dtype)


def _attn_prompt(qt, kb, vt, bias_t, head_dim, hpb, nb):
    AW, S = qt.shape
    assert head_dim == LANES and hpb * nb <= LANES
    tq = MOBA_BLOCK
    gw = hpb * head_dim
    return pl.pallas_call(
        functools.partial(_attn_prompt_kernel, nb=nb),
        grid=(AW // gw, S // tq),
        in_specs=[pl.BlockSpec((gw, tq), lambda h, i: (h, i)),
                  pl.BlockSpec((S, gw), lambda h, i: (0, h)),
                  pl.BlockSpec((nb, gw, MOBA_BLOCK), lambda h, i: (0, h, 0)),
                  pl.BlockSpec((hpb * nb, tq), lambda h, i: (h, i))],
        out_specs=pl.BlockSpec((tq, gw), lambda h, i: (i, h)),
        out_shape=jax.ShapeDtypeStruct((S, AW), _BF16),
        scratch_shapes=[pltpu.VMEM((hpb, 2 * LANES, tq), _BF16),
                        pltpu.VMEM((hpb, MOBA_BLOCK, tq), _F32),
                        pltpu.VMEM((hpb, MOBA_BLOCK, tq), _F32),
                        pltpu.VMEM((hpb, 1, tq), _F32), pltpu.VMEM((hpb, 1, tq), _F32),
                        pltpu.VMEM((hpb, head_dim, tq), _F32)],
        compiler_params=_params(2),
        name="moba_prompt",
    )(qt, kb, vt, bias_t)


PAGES_PER_BLOCK = 2


def _attn_sample_kernel(pt_ref, q_ref, kn_ref, vn_ref, *refs, n_blocks, blocks_per_step,
                        scale_log2e):
    del pt_ref
    n_pages = PAGES_PER_BLOCK * blocks_per_step
    k_refs, v_refs = refs[:n_pages], refs[n_pages:2 * n_pages]
    o_ref, m_ref, l_ref, g_ref, acc_ref = refs[2 * n_pages:]
    step = pl.program_id(1)
    page, H, Dh = k_refs[0].shape
    rows = page * H
    q = q_ref[...]
    q_hi = q.astype(_BF16)
    q_lo = (q - q_hi.astype(_F32)).astype(_BF16)
    q2 = jnp.concatenate([q_hi, q_lo], axis=0)

    def scores(k_ref):
        k16 = k_ref[...].reshape(rows, Dh).astype(_BF16)
        s2 = lax.dot_general(q2, k16, _NT, preferred_element_type=_F32)
        return s2[0:H] + s2[H:2 * H]

    head_of_row = lax.broadcasted_iota(jnp.int32, (H, PAGES_PER_BLOCK * rows), 1) % H
    own = head_of_row == lax.broadcasted_iota(jnp.int32, (H, PAGES_PER_BLOCK * rows), 0)
    for t in range(blocks_per_step):
        k0_ref, k1_ref = k_refs[2 * t], k_refs[2 * t + 1]
        v0_ref, v1_ref = v_refs[2 * t], v_refs[2 * t + 1]
        s = jnp.concatenate([scores(k0_ref), scores(k1_ref)], axis=1)
        sm = jnp.where(own, s, -jnp.inf)
        m = jnp.max(sm, axis=1, keepdims=True)
        p = jnp.exp2((sm - m) * scale_log2e)
        l = jnp.sum(p, axis=1, keepdims=True)
        g = jnp.sum(jnp.where(own, s, 0.0), axis=1, keepdims=True)
        p16 = p.astype(_BF16)
        o = (jnp.dot(p16[:, 0:rows], v0_ref[...].reshape(rows, Dh).astype(_BF16),
                     preferred_element_type=_F32)
             + jnp.dot(p16[:, rows:2 * rows], v1_ref[...].reshape(rows, Dh).astype(_BF16),
                       preferred_element_type=_F32))
        n = step * blocks_per_step + t
        m_ref[n] = jnp.broadcast_to(m, (H, LANES))
        l_ref[n] = jnp.broadcast_to(l, (H, LANES))
        g_ref[n] = jnp.broadcast_to(g, (H, LANES))
        acc_ref[n] = o

    @pl.when(step == n_blocks // blocks_per_step - 1)
    def _():
        gates = [g_ref[b] for b in range(n_blocks)]
        s_own = jnp.sum(q * kn_ref[...], axis=1, keepdims=True)
        m_tot = jnp.broadcast_to(s_own, (H, LANES))
        sel = []
        for b in range(n_blocks):
            rank = jnp.zeros((H, LANES), _F32)
            for c in range(n_blocks):
                if c == b:
                    continue
                ahead = gates[c] >= gates[b] if c < b else gates[c] > gates[b]
                rank = rank + jnp.where(ahead, 1.0, 0.0)
            sel.append(rank < float(MOBA_TOPK))
            m_tot = jnp.where(sel[b], jnp.maximum(m_tot, m_ref[b]), m_tot)
        w_own = jnp.exp2((s_own - m_tot) * scale_log2e)
        den = w_own
        num = w_own * vn_ref[...]
        for b in range(n_blocks):
            w = jnp.where(sel[b], jnp.exp2((m_ref[b] - m_tot) * scale_log2e), 0.0)
            den = den + w * l_ref[b]
            num = num + w * acc_ref[b]
        o_ref[...] = num / den


def _attn_sample(q3, kn3, vn3, cache_k, cache_v, page_table, past_len):
    Bd, H, Dh = q3.shape
    page = cache_k.shape[1]
    assert Dh == LANES and MOBA_BLOCK == PAGES_PER_BLOCK * page
    assert past_len % MOBA_BLOCK == 0
    n_blocks = past_len // MOBA_BLOCK
    assert n_blocks >= MOBA_TOPK
    bps = _largest_tile(n_blocks, (4, 2, 1))
    pps = PAGES_PER_BLOCK * bps
    n_pages = page_table.shape[1]
    pt = page_table.reshape(-1)
    tok = pl.BlockSpec((None, H, Dh), lambda b, n, pt: (b, 0, 0))

    def page_spec(which):
        return pl.BlockSpec((None, page, H, Dh),
                            lambda b, n, pt: (pt[b * n_pages + pps * n + which], 0, 0, 0))

    pages = [page_spec(w) for w in range(pps)]
    grid_spec = pltpu.PrefetchScalarGridSpec(
        num_scalar_prefetch=1,
        grid=(Bd, n_blocks // bps),
        in_specs=[tok, tok, tok] + pages + pages,
        out_specs=tok,
        scratch_shapes=[pltpu.VMEM((n_blocks, H, LANES), _F32),
                        pltpu.VMEM((n_blocks, H, LANES), _F32),
                        pltpu.VMEM((n_blocks, H, LANES), _F32),
                        pltpu.VMEM((n_blocks, H, Dh), _F32)],
    )
    return pl.pallas_call(
        functools.partial(_attn_sample_kernel, n_blocks=n_blocks, blocks_per_step=bps,
                          scale_log2e=Dh ** -0.5 * math.log2(math.e)),
        grid_spec=grid_spec,
        out_shape=jax.ShapeDtypeStruct((Bd, H, Dh), _F32),
        compiler_params=_params(2),
        name="moba_sample",
    )(pt, q3, kn3, vn3, *([cache_k] * pps), *([cache_v] * pps))


def _oproj_kernel(ap_ref, as_ref, xp_ref, xs_ref, wo_ref, g_ref, b_ref, op_ref, os_ref,
                  *, npt, alpha):
    i = pl.program_id(0)

    def body(a_ref, x_ref, o_ref):
        m = jnp.dot(a_ref[...].astype(_BF16), wo_ref[...], preferred_element_type=_F32)
        o_ref[...] = _layer_norm(alpha * x_ref[...] + m, g_ref[...], b_ref[...])

    _dual(i, npt, lambda: body(ap_ref, xp_ref, op_ref), lambda: body(as_ref, xs_ref, os_ref))


def _oproj(ap, a_s, xp, xs, wo, g4, b4, layer, alpha, tm):
    S, D = xp.shape
    ns = xs.shape[0]
    AW = ap.shape[1]
    npt = S // tm
    vec = pl.BlockSpec((None, None, 1, D), lambda i: (layer, 1, 0, 0))
    return pl.pallas_call(
        functools.partial(_oproj_kernel, npt=npt, alpha=alpha),
        grid=(npt + 1,),
        in_specs=[pl.BlockSpec((tm, AW), _prompt_rows(npt)),
                  pl.BlockSpec((ns, AW), lambda i: (0, 0)),
                  pl.BlockSpec((tm, D), _prompt_rows(npt)),
                  pl.BlockSpec((ns, D), lambda i: (0, 0)),
                  pl.BlockSpec((None, AW, D), lambda i: (0, 0, 0)), vec, vec],
        out_specs=[pl.BlockSpec((tm, D), _prompt_rows(npt)),
                   pl.BlockSpec((ns, D), lambda i: (0, 0))],
        out_shape=[jax.ShapeDtypeStruct((S, D), _F32), jax.ShapeDtypeStruct((ns, D), _F32)],
        compiler_params=_params(1),
        name="oproj_postnorm",
    )(ap, a_s, xp, xs, wo, g4, b4)


def kernel(x_prompt, x_sample, state_conv, cache_k, cache_v, page_table, ffn_w_gate, ffn_w_up, ffn_w_down, ln_g, ln_b, conv_w_pw1, conv_b_pw1, conv_w_dw, conv_b_dw, conv_ln_g, conv_ln_b, conv_w_pw2, conv_b_pw2, attn_w_q, attn_w_o, w_kv):
    B, S, D = x_prompt.shape
    Bd, Sd, _ = x_sample.shape
    depth = ffn_w_gate.shape[0]
    n_phys, page, H, Dh = cache_k.shape
    past_len = page_table.shape[1] * page
    conv_w = conv_w_dw.shape[1]
    assert B == 1 and Sd == 1 and depth == 2 and conv_w_pw1.shape[0] == 1
    assert S % MOBA_BLOCK == 0 and H * Dh == attn_w_q.shape[-1]
    alpha = (2.0 * depth) ** 0.25
    tm = _largest_tile(S, (512, 256))
    tm_ffn = _largest_tile(S, (1024, 512, 256))

    wg, wu, wd = (w.astype(_BF16) for w in (ffn_w_gate, ffn_w_up, ffn_w_down))
    g4, b4 = ln_g[:, :, None, :], ln_b[:, :, None, :]
    half = Dh // 2
    inv = ROPE_THETA ** (-jnp.arange(half, dtype=_F32) / half)
    inv128 = jnp.concatenate([inv, inv])[None, :]

    xp, xs = x_prompt.reshape(S, D), x_sample.reshape(Bd, D)

    xp, xs = _ffn(xp, xs, wg, wu, wd, g4, b4, 0, 0, alpha, tm_ffn)
    up, us = _glu(xp, xs, conv_w_pw1.astype(_BF16), conv_b_pw1, tm)
    xp, xs = _conv_mixer(up, us, state_conv[0], xp, xs, conv_w_dw[0], conv_b_dw, conv_ln_g,
                         conv_ln_b, conv_w_pw2[0].astype(_BF16), conv_b_pw2, ln_g[0, 1:2],
                         ln_b[0, 1:2], alpha)
    xp, xs = _ffn(xp, xs, wg, wu, wd, g4, b4, 0, 1, alpha, tm_ffn)
    kp, vp, kbp, vt, kmean, ks, vs = _kv(xp, xs, w_kv.astype(_BF16), inv128, past_len, tm)

    xp, xs = _ffn(xp, xs, wg, wu, wd, g4, b4, 1, 0, alpha, tm_ffn)
    nb = S // MOBA_BLOCK
    qt, bias_t, qs = _q_proj(xp, xs, attn_w_q.astype(_BF16), inv128, kmean, past_len, tm)
    hpb = _q_col_tile(H * Dh) // LANES
    ap = _attn_prompt(qt, kbp, vt, bias_t, Dh, hpb, nb)
    a_s = _attn_sample(qs.reshape(Bd, H, Dh), ks.reshape(Bd, H, Dh), vs.reshape(Bd, H, Dh),
                       cache_k, cache_v, page_table, past_len)
    xp, xs = _oproj(ap, a_s.reshape(Bd, H * Dh), xp, xs, attn_w_o.astype(_BF16), g4, b4, 1,
                    alpha, tm)
    xp, xs = _ffn(xp, xs, wg, wu, wd, g4, b4, 1, 1, alpha, tm_ffn)

    conv_p = up[S - (conv_w - 1):].reshape(1, 1, conv_w - 1, D)
    conv_s = jnp.concatenate([state_conv[:, :, 1:], us[None, :, None, :]], axis=2)
    return (xp.reshape(1, S, D), xs.reshape(Bd, 1, D), conv_p, conv_s,
            kp.reshape(1, S, H, Dh), vp.reshape(1, S, H, Dh),
            ks.reshape(Bd, 1, H, Dh), vs.reshape(Bd, 1, H, Dh))
```

```python
import functools
import math

import jax
import jax.numpy as jnp
from jax import lax
from jax.experimental import pallas as pl
from jax.experimental.pallas import tpu as pltpu

_F32 = jnp.float32
_BF16 = jnp.bfloat16

MOBA_BLOCK = 256
MOBA_TOPK = 3
ROPE_THETA = 10000.0
LN_EPS = 1e-5
LANES = 128
MASK_BIAS = -1e30
V7X_VMEM_LIMIT = 60 * 1024 * 1024

_NT = (((1,), (1,)), ((), ()))


def _params(n_grid_dims, vmem_bytes=V7X_VMEM_LIMIT):
    return pltpu.CompilerParams(
        dimension_semantics=("arbitrary",) * n_grid_dims, vmem_limit_bytes=vmem_bytes)


def _largest_tile(n, candidates):
    for c in candidates:
        if n % c == 0:
            return c
    return n


def _layer_norm(z, g, b):
    mu = jnp.mean(z, axis=-1, keepdims=True)
    zc = z - mu
    var = jnp.mean(zc * zc, axis=-1, keepdims=True)
    return zc * lax.rsqrt(var + LN_EPS) * g + b


def _sigmoid(a):
    return 1.0 / (1.0 + jnp.exp(-a))


def _dual(i, n_prompt_tiles, prompt_fn, sample_fn):
    pl.when(i < n_prompt_tiles)(prompt_fn)
    pl.when(i == n_prompt_tiles)(sample_fn)


def _prompt_rows(npt):
    return lambda i, *_: (jnp.minimum(i, npt - 1), 0)


def _prompt_tile(npt, nj):
    return lambda i, j: (jnp.minimum(i, npt - 1), jnp.where(i < npt, j, nj - 1))


def _sample_tile(npt):
    return lambda i, j: (0, jnp.where(i < npt, 0, j))


def _ffn_kernel(xp_ref, xs_ref, wg_ref, wu_ref, wd_ref, g_ref, b_ref, *refs, npt, ns, alpha,
                n_cast):
    cast_in, refs = refs[:n_cast], refs[n_cast:]
    op_ref, os_ref = refs[:2]
    cast_out, xb_ref = refs[2:2 + n_cast], refs[2 + n_cast]
    i, j, nj = pl.program_id(0), pl.program_id(1), pl.num_programs(1)

    def body(x_ref, o_ref, xb):
        @pl.when(j == 0)
        def _():
            xb[...] = x_ref[...].astype(_BF16)
            o_ref[...] = jnp.zeros_like(o_ref)

        x16 = xb[...]
        a = jnp.dot(x16, wg_ref[...], preferred_element_type=_F32)
        u = jnp.dot(x16, wu_ref[...], preferred_element_type=_F32)
        h = (a * _sigmoid(a)) * u
        o_ref[...] += jnp.dot(h.astype(_BF16), wd_ref[...], preferred_element_type=_F32)

        @pl.when(j == nj - 1)
        def _():
            z = alpha * x_ref[...] + 0.5 * o_ref[...]
            o_ref[...] = _layer_norm(z, g_ref[...], b_ref[...])

    def prompt_step():
        body(xp_ref, op_ref, xb_ref)
        for src, dst in zip(cast_in, cast_out):
            dst[...] = src[...].astype(_BF16)

    _dual(i, npt, prompt_step, lambda: body(xs_ref, os_ref, xb_ref.at[pl.ds(0, ns)]))


def _ffn(xp, xs, w16, g4, b4, layer, which, alpha, tm, next_f32=None):
    norm = 2 * which
    wg, wu, wd = w16
    S, D = xp.shape
    ns = xs.shape[0]
    F = wg.shape[-1]
    tf = _largest_tile(F, (512, 256, 128))
    npt, nj = S // tm, F // tf
    w_in = pl.BlockSpec((D, tf), lambda i, j: (0, j))
    w_out = pl.BlockSpec((tf, D), lambda i, j: (j, 0))
    vec = pl.BlockSpec((None, None, 1, D), lambda i, j: (layer, norm, 0, 0))
    single = pl.Buffered(1)
    cast_args, cast_in, cast_out, cast_shapes = [], [], [], []
    if next_f32 is not None:
        ng, nu, nd, nl, nw = next_f32
        rows = D // npt
        assert rows % LANES == 0
        tile = _prompt_tile(npt, nj)
        cast_args = [ng, nu, nd]
        in_map = lambda i, j: (nl, nw) + tile(i, j)
        in_map_t = lambda i, j: (nl, nw) + tile(i, j)[::-1]
        cast_in = [pl.BlockSpec((None, None, rows, tf), in_map),
                   pl.BlockSpec((None, None, rows, tf), in_map),
                   pl.BlockSpec((None, None, tf, rows), in_map_t)]
        cast_out = [pl.BlockSpec((rows, tf), tile), pl.BlockSpec((rows, tf), tile),
                    pl.BlockSpec((tf, rows), lambda i, j: tile(i, j)[::-1])]
        cast_shapes = [jax.ShapeDtypeStruct((D, F), _BF16), jax.ShapeDtypeStruct((D, F), _BF16),
                       jax.ShapeDtypeStruct((F, D), _BF16)]
    outs = pl.pallas_call(
        functools.partial(_ffn_kernel, npt=npt, ns=ns, alpha=alpha, n_cast=len(cast_args)),
        grid=(npt + 1, nj),
        in_specs=[pl.BlockSpec((tm, D), _prompt_rows(npt), pipeline_mode=single),
                  pl.BlockSpec((ns, D), lambda i, j: (0, 0), pipeline_mode=single),
                  w_in, w_in, w_out, vec, vec] + cast_in,
        out_specs=[pl.BlockSpec((tm, D), _prompt_rows(npt), pipeline_mode=single),
                   pl.BlockSpec((ns, D), lambda i, j: (0, 0), pipeline_mode=single)] + cast_out,
        out_shape=[jax.ShapeDtypeStruct((S, D), _F32),
                   jax.ShapeDtypeStruct((ns, D), _F32)] + cast_shapes,
        scratch_shapes=[pltpu.VMEM((tm, D), _BF16)],
        compiler_params=_params(2),
        name="ffn_postnorm",
    )(xp, xs, wg, wu, wd, g4, b4, *cast_args)
    return outs[0], outs[1], tuple(outs[2:])


def _glu_kernel(xp_ref, xs_ref, wa_ref, wb_ref, ba_ref, bb_ref, op_ref, os_ref, xb_ref,
                *, npt, ns):
    i, j = pl.program_id(0), pl.program_id(1)

    def body(x_ref, o_ref, xb):
        @pl.when(j == 0)
        def _():
            xb[...] = x_ref[...].astype(_BF16)

        x16 = xb[...]
        a = jnp.dot(x16, wa_ref[...], preferred_element_type=_F32) + ba_ref[...]
        gate = jnp.dot(x16, wb_ref[...], preferred_element_type=_F32) + bb_ref[...]
        o_ref[...] = a * _sigmoid(gate)

    _dual(i, npt, lambda: body(xp_ref, op_ref, xb_ref),
          lambda: body(xs_ref, os_ref, xb_ref.at[pl.ds(0, ns)]))


def _glu(xp, xs, w_pw1, b_pw1, tm):
    S, D = xp.shape
    ns = xs.shape[0]
    tn = _largest_tile(D, (512, 256, 128))
    npt, nj = S // tm, D // tn
    b3 = b_pw1.reshape(1, 1, 2 * D)
    return pl.pallas_call(
        functools.partial(_glu_kernel, npt=npt, ns=ns),
        grid=(npt + 1, nj),
        in_specs=[pl.BlockSpec((tm, D), _prompt_rows(npt)),
                  pl.BlockSpec((ns, D), lambda i, j: (0, 0)),
                  pl.BlockSpec((None, D, tn), lambda i, j: (0, 0, j)),
                  pl.BlockSpec((None, D, tn), lambda i, j: (0, 0, j + nj)),
                  pl.BlockSpec((None, 1, tn), lambda i, j: (0, 0, j)),
                  pl.BlockSpec((None, 1, tn), lambda i, j: (0, 0, j + nj))],
        out_specs=[pl.BlockSpec((tm, tn), _prompt_tile(npt, nj)),
                   pl.BlockSpec((ns, tn), _sample_tile(npt))],
        out_shape=[jax.ShapeDtypeStruct((S, D), _F32), jax.ShapeDtypeStruct((ns, D), _F32)],
        scratch_shapes=[pltpu.VMEM((tm, D), _BF16)],
        compiler_params=_params(2),
        name="pw1_glu",
    )(xp, xs, w_pw1, w_pw1, b3, b3)


def _conv_tail(y, x1, gcn_ref, bcn_ref, w2_ref, b2_ref, g_ref, b_ref, alpha):
    z = _layer_norm(y, gcn_ref[...], bcn_ref[...])
    z = z * _sigmoid(z)
    m = jnp.dot(z.astype(_BF16), w2_ref[...], preferred_element_type=_F32) + b2_ref[...]
    return _layer_norm(alpha * x1 + m, g_ref[...], b_ref[...])


CONV_HALO = 32
CONV_ROWS = 64
CONV_COLS = 256
SUBLANES = 8


def _conv_prompt_kernel(u_ref, halo_ref, x1_ref, wdw_ref, bdw_ref, gcn_ref, bcn_ref, w2_ref,
                        b2_ref, g_ref, b_ref, o_ref, ext_ref, y_ref, wb_ref, *, conv_w, alpha):
    i = pl.program_id(0)
    tm, D = u_ref.shape

    @pl.when(i == 0)
    def _():
        for k in range(conv_w):
            wb_ref[k] = jnp.broadcast_to(wdw_ref[k:k + 1, :], (SUBLANES, D))

    ext_ref[0:CONV_HALO, :] = jnp.where(i == 0, 0.0, halo_ref[...])
    ext_ref[CONV_HALO:, :] = u_ref[...]
    lead = CONV_HALO - (conv_w - 1)
    cc = min(CONV_COLS, D)

    def row_chunk(r, carry):
        r0 = pl.multiple_of(r * CONV_ROWS, CONV_ROWS)
        for c in range(D // cc):
            cols = slice(c * cc, (c + 1) * cc)
            base = ext_ref[pl.ds(r0, CONV_ROWS + CONV_HALO), cols]
            acc = jnp.zeros((CONV_ROWS // SUBLANES, SUBLANES, cc), _F32)
            for shift in range(SUBLANES):
                offs = [o for o in range(shift, lead + conv_w, SUBLANES) if o >= lead]
                if not offs:
                    continue
                win = base[offs[0]:offs[-1] + CONV_ROWS]
                for o in offs:
                    tap = win[o - offs[0]:o - offs[0] + CONV_ROWS]
                    tap = tap.reshape(CONV_ROWS // SUBLANES, SUBLANES, cc)
                    acc = acc + wb_ref[o - lead, :, cols][None] * tap
            y_ref[pl.ds(r0, CONV_ROWS), cols] = acc.reshape(CONV_ROWS, cc) + bdw_ref[:, cols]
        return carry

    lax.fori_loop(0, tm // CONV_ROWS, row_chunk, 0)
    o_ref[...] = _conv_tail(y_ref[...], x1_ref[...], gcn_ref, bcn_ref, w2_ref, b2_ref,
                            g_ref, b_ref, alpha)


def _conv_sample_kernel(st_ref, u_ref, x1_ref, wdw_ref, bdw_ref, gcn_ref, bcn_ref, w2_ref,
                        b2_ref, g_ref, b_ref, o_ref, ost_ref, *, conv_w, alpha):
    w = wdw_ref[...]
    u = u_ref[...]
    y = jnp.sum(st_ref[...] * w[None, 0:conv_w - 1, :], axis=1)
    y = y + u * w[conv_w - 1:conv_w, :] + bdw_ref[...]
    o_ref[...] = _conv_tail(y, x1_ref[...], gcn_ref, bcn_ref, w2_ref, b2_ref, g_ref, b_ref,
                            alpha)
    ost_ref[:, 0:conv_w - 2, :] = st_ref[:, 1:conv_w - 1, :]
    ost_ref[:, conv_w - 2:conv_w - 1, :] = u[:, None, :]


def _conv_mixer(up, us, state, x1p, x1s, wdw, bdw, gcn, bcn, w2, b2, g, b, alpha):
    S, D = up.shape
    ns = us.shape[0]
    conv_w = wdw.shape[0]
    assert conv_w - 1 <= CONV_HALO
    tm = _largest_tile(S, (256, 128))
    per_halo = tm // CONV_HALO
    full = lambda shape: pl.BlockSpec(shape, lambda i: (0,) * len(shape))
    shared = [full((conv_w, D)), full((1, D)), full((1, D)), full((1, D)), full((D, D)),
              full((1, D)), full((1, D)), full((1, D))]
    shared_args = (wdw, bdw, gcn, bcn, w2, b2, g, b)
    xp = pl.pallas_call(
        functools.partial(_conv_prompt_kernel, conv_w=conv_w, alpha=alpha),
        grid=(S // tm,),
        in_specs=[pl.BlockSpec((tm, D), lambda i: (i, 0)),
                  pl.BlockSpec((CONV_HALO, D), lambda i: (jnp.maximum(i * per_halo - 1, 0), 0)),
                  pl.BlockSpec((tm, D), lambda i: (i, 0))] + shared,
        out_specs=pl.BlockSpec((tm, D), lambda i: (i, 0)),
        out_shape=jax.ShapeDtypeStruct((S, D), _F32),
        scratch_shapes=[pltpu.VMEM((CONV_HALO + tm, D), _F32), pltpu.VMEM((tm, D), _F32),
                        pltpu.VMEM((conv_w, SUBLANES, D), _F32)],
        compiler_params=_params(1),
        name="conv_prompt",
    )(up, up, x1p, *shared_args)
    tb = _largest_tile(ns, (16, 8))
    state_spec = pl.BlockSpec((tb, conv_w - 1, D), lambda i: (i, 0, 0))
    xs, new_state = pl.pallas_call(
        functools.partial(_conv_sample_kernel, conv_w=conv_w, alpha=alpha),
        grid=(ns // tb,),
        in_specs=[state_spec,
                  pl.BlockSpec((tb, D), lambda i: (i, 0)),
                  pl.BlockSpec((tb, D), lambda i: (i, 0))] + shared,
        out_specs=[pl.BlockSpec((tb, D), lambda i: (i, 0)), state_spec],
        out_shape=[jax.ShapeDtypeStruct((ns, D), _F32),
                   jax.ShapeDtypeStruct((ns, conv_w - 1, D), _F32)],
        compiler_params=_params(1),
        name="conv_sample",
    )(state, us, x1s, *shared_args)
    return xp, xs, new_state


def _rope_tables(pos, inv_ref, cos_ref, sin_ref):
    ang = pos * inv_ref[...]
    lane = lax.broadcasted_iota(jnp.int32, ang.shape, 1)
    cos_ref[...] = jnp.cos(ang)
    sin_ref[...] = jnp.where(lane < LANES // 2, -1.0, 1.0) * jnp.sin(ang)


def _rope_head(x, cos, sin):
    return x * cos + pltpu.roll(x, LANES // 2, axis=1) * sin


def _row_positions(i, tm):
    return (i * tm + lax.broadcasted_iota(jnp.int32, (tm, LANES), 0)).astype(_F32)


def _kv_kernel(xp_ref, xs_ref, wk_ref, wv_ref, inv_ref,
               kp_ref, vp_ref, kbp_ref, vbp_ref, km_ref, ks_ref, vs_ref,
               xb_ref, cos_ref, sin_ref, *, npt, ns, past_len):
    i, j = pl.program_id(0), pl.program_id(1)
    tn = wk_ref.shape[1]

    def body(x_ref, xb, cos_r, sin_r, pos_fn, k_ref, v_ref, kb_ref, vb_ref, kmean_ref):
        rows = x_ref.shape[0]

        @pl.when(j == 0)
        def _():
            xb[...] = x_ref[...].astype(_BF16)
            _rope_tables(pos_fn(rows), inv_ref, cos_r, sin_r)

        x16 = xb[...]
        k = jnp.dot(x16, wk_ref[...], preferred_element_type=_F32)
        v = jnp.dot(x16, wv_ref[...], preferred_element_type=_F32)
        cos, sin = cos_r[...], sin_r[...]
        k = jnp.concatenate(
            [_rope_head(k[:, h * LANES:(h + 1) * LANES], cos, sin) for h in range(tn // LANES)],
            axis=1)
        k_ref[...] = k
        v_ref[...] = v
        if kb_ref is not None:
            kb_ref[...] = k.astype(_BF16)
            nblk = rows // MOBA_BLOCK
            for b in range(nblk):
                vb_ref[b] = v[b * MOBA_BLOCK:(b + 1) * MOBA_BLOCK].T.astype(_BF16)
            means = [jnp.mean(k[b * MOBA_BLOCK:(b + 1) * MOBA_BLOCK], axis=0, keepdims=True)
                     for b in range(nblk)]
            means.append(jnp.zeros((kmean_ref.shape[0] - nblk, tn), _F32))
            kmean_ref[...] = jnp.concatenate(means, axis=0)

    _dual(i, npt,
          lambda: body(xp_ref, xb_ref, cos_ref, sin_ref, lambda rows: _row_positions(i, rows),
                       kp_ref, vp_ref, kbp_ref, vbp_ref, km_ref),
          lambda: body(xs_ref, xb_ref.at[pl.ds(0, ns)], cos_ref.at[pl.ds(0, ns)],
                       sin_ref.at[pl.ds(0, ns)],
                       lambda rows: jnp.full((rows, LANES), float(past_len), _F32),
                       ks_ref, vs_ref, None, None, None))


KMEAN_ROWS = 8


def _kv(xp, xs, w_kv, inv128, past_len, tm):
    S, D = xp.shape
    ns = xs.shape[0]
    AW = w_kv.shape[1] // 2
    tn = _largest_tile(AW, (512, 256, 128))
    npt, nj = S // tm, AW // tn
    assert tm % MOBA_BLOCK == 0 and tm // MOBA_BLOCK <= KMEAN_ROWS
    prow, srow = _prompt_tile(npt, nj), _sample_tile(npt)
    outs = pl.pallas_call(
        functools.partial(_kv_kernel, npt=npt, ns=ns, past_len=past_len),
        grid=(npt + 1, nj),
        in_specs=[pl.BlockSpec((tm, D), _prompt_rows(npt)),
                  pl.BlockSpec((ns, D), lambda i, j: (0, 0)),
                  pl.BlockSpec((D, tn), lambda i, j: (0, j)),
                  pl.BlockSpec((D, tn), lambda i, j: (0, j + nj)),
                  pl.BlockSpec((1, LANES), lambda i, j: (0, 0))],
        out_specs=[pl.BlockSpec((tm, tn), prow), pl.BlockSpec((tm, tn), prow),
                   pl.BlockSpec((tm, tn), prow),
                   pl.BlockSpec((tm // MOBA_BLOCK, tn, MOBA_BLOCK),
                                lambda i, j: (prow(i, j)[0], prow(i, j)[1], 0)),
                   pl.BlockSpec((None, KMEAN_ROWS, tn),
                                lambda i, j: (prow(i, j)[0], 0, prow(i, j)[1])),
                   pl.BlockSpec((ns, tn), srow), pl.BlockSpec((ns, tn), srow)],
        out_shape=[jax.ShapeDtypeStruct((S, AW), _F32), jax.ShapeDtypeStruct((S, AW), _F32),
                   jax.ShapeDtypeStruct((S, AW), _BF16),
                   jax.ShapeDtypeStruct((S // MOBA_BLOCK, AW, MOBA_BLOCK), _BF16),
                   jax.ShapeDtypeStruct((npt, KMEAN_ROWS, AW), _F32),
                   jax.ShapeDtypeStruct((ns, AW), _F32), jax.ShapeDtypeStruct((ns, AW), _F32)],
        scratch_shapes=[pltpu.VMEM((tm, D), _BF16), pltpu.VMEM((tm, LANES), _F32),
                        pltpu.VMEM((tm, LANES), _F32)],
        compiler_params=_params(2),
        name="kv_rope",
    )(xp, xs, w_kv, w_kv, inv128)
    kp, vp, kbp, vtp, km3, ks, vs = outs
    kmean = km3[:, :tm // MOBA_BLOCK].reshape(S // MOBA_BLOCK, AW)
    return kp, vp, kbp, vtp, kmean, ks, vs


def _select_bias(gate_t, query_block):
    nb = gate_t.shape[0]
    blk = lax.broadcasted_iota(jnp.int32, gate_t.shape, 0).astype(_F32)
    g = jnp.where(blk < query_block, gate_t, -jnp.inf)
    sel = jnp.zeros(gate_t.shape, jnp.bool_)
    for _ in range(MOBA_TOPK):
        mx = jnp.max(g, axis=0, keepdims=True)
        is_max = jnp.logical_and(g == mx, g > -jnp.inf)
        first = jnp.min(jnp.where(is_max, blk, float(nb)), axis=0, keepdims=True)
        pick = blk == first
        sel = jnp.logical_or(sel, pick)
        g = jnp.where(pick, -jnp.inf, g)
    return jnp.where(sel, 0.0, MASK_BIAS)


def _q_kernel(xp_ref, xs_ref, wq_ref, inv_ref, km_ref, qt_ref, bias_ref, qs_ref,
              xb_ref, cos_ref, sin_ref, *, npt, ns, past_len, score_scale):
    i, j = pl.program_id(0), pl.program_id(1)
    tn = wq_ref.shape[1]
    nb = km_ref.shape[0]

    def body(x_ref, xb, cos_r, sin_r, pos_fn, prompt):
        rows = x_ref.shape[0]

        @pl.when(j == 0)
        def _():
            xb[...] = x_ref[...].astype(_BF16)
            _rope_tables(pos_fn(rows), inv_ref, cos_r, sin_r)

        q = jnp.dot(xb[...], wq_ref[...], preferred_element_type=_F32)
        cos, sin = cos_r[...], sin_r[...]
        heads = [_rope_head(q[:, h * LANES:(h + 1) * LANES], cos, sin)
                 for h in range(tn // LANES)]
        q = jnp.concatenate(heads, axis=1)
        if not prompt:
            qs_ref[...] = q
            return
        qt_ref[...] = (q * score_scale).T.astype(_BF16)
        query_block = (i * rows + lax.broadcasted_iota(jnp.int32, (1, rows), 1)) // MOBA_BLOCK
        query_block = query_block.astype(_F32)
        for h, qh in enumerate(heads):
            gate_t = lax.dot_general(km_ref[:, h * LANES:(h + 1) * LANES], qh, _NT,
                                     preferred_element_type=_F32,
                                     precision=lax.Precision.HIGHEST)
            bias_ref[h * nb:(h + 1) * nb, :] = _select_bias(gate_t, query_block)

    _dual(i, npt,
          lambda: body(xp_ref, xb_ref, cos_ref, sin_ref, lambda rows: _row_positions(i, rows),
                       True),
          lambda: body(xs_ref, xb_ref.at[pl.ds(0, ns)], cos_ref.at[pl.ds(0, ns)],
                       sin_ref.at[pl.ds(0, ns)],
                       lambda rows: jnp.full((rows, LANES), float(past_len), _F32), False))


def _q_col_tile(attn_width):
    return _largest_tile(attn_width, (512, 256, 128))


def _q_proj(xp, xs, wq, inv128, kmean, past_len, tm):
    S, D = xp.shape
    ns = xs.shape[0]
    AW = wq.shape[-1]
    nb = kmean.shape[0]
    tn = _q_col_tile(AW)
    hpb = tn // LANES
    npt, nj = S // tm, AW // tn
    assert nj == 1 or (hpb * nb) % SUBLANES == 0
    prow = _prompt_tile(npt, nj)
    pcol = lambda i, j: prow(i, j)[::-1]
    return pl.pallas_call(
        functools.partial(_q_kernel, npt=npt, ns=ns, past_len=past_len,
                          score_scale=LANES ** -0.5 * math.log2(math.e)),
        grid=(npt + 1, nj),
        in_specs=[pl.BlockSpec((tm, D), _prompt_rows(npt)),
                  pl.BlockSpec((ns, D), lambda i, j: (0, 0)),
                  pl.BlockSpec((None, D, tn), lambda i, j: (0, 0, j)),
                  pl.BlockSpec((1, LANES), lambda i, j: (0, 0)),
                  pl.BlockSpec((nb, tn), lambda i, j: (0, j))],
        out_specs=[pl.BlockSpec((tn, tm), pcol),
                   pl.BlockSpec((hpb * nb, tm), pcol),
                   pl.BlockSpec((ns, tn), _sample_tile(npt))],
        out_shape=[jax.ShapeDtypeStruct((AW, S), _BF16),
                   jax.ShapeDtypeStruct(((AW // LANES) * nb, S), _F32),
                   jax.ShapeDtypeStruct((ns, AW), _F32)],
        scratch_shapes=[pltpu.VMEM((tm, D), _BF16), pltpu.VMEM((tm, LANES), _F32),
                        pltpu.VMEM((tm, LANES), _F32)],
        compiler_params=_params(2),
        name="q_rope_select",
    )(xp, xs, wq, inv128, kmean)


def _attn_prompt_kernel(qt_ref, k_ref, vt_ref, bias_ref, o_ref, qa_ref, sa_ref, sb_ref, m_ref,
                        l_ref, acc_ref, *, nb):
    qi = pl.program_id(1)
    tq = qt_ref.shape[1]
    n_heads = qt_ref.shape[0] // LANES

    bias = bias_ref[...].astype(_BF16)
    if bias.shape[0] < LANES:
        bias = jnp.concatenate(
            [bias, jnp.zeros((LANES - bias.shape[0], tq), _BF16)], axis=0)
    key_i = lax.broadcasted_iota(jnp.int32, (MOBA_BLOCK, tq), 0)
    qry_i = lax.broadcasted_iota(jnp.int32, (MOBA_BLOCK, tq), 1)
    j0 = pl.multiple_of(qi * MOBA_BLOCK, MOBA_BLOCK)
    head_cols = [slice(g * LANES, (g + 1) * LANES) for g in range(n_heads)]
    scores = [jnp.dot(k_ref[pl.ds(j0, MOBA_BLOCK), cols], qt_ref[cols, :],
                      preferred_element_type=_F32) for cols in head_cols]
    probs = []
    for g in range(n_heads):
        s = jnp.where(key_i <= qry_i, scores[g], -jnp.inf)
        m = jnp.max(s, axis=0, keepdims=True)
        p = jnp.exp2(s - m)
        m_ref[g] = m
        l_ref[g] = jnp.sum(p, axis=0, keepdims=True)
        probs.append(p.astype(_BF16))
    for g, cols in enumerate(head_cols):
        acc_ref[g] = jnp.dot(vt_ref[qi, cols, :], probs[g], preferred_element_type=_F32)
        qa_ref[g] = jnp.concatenate([qt_ref[cols, :], bias], axis=0)

    lane = lax.broadcasted_iota(jnp.int32, (MOBA_BLOCK, LANES), 1)

    def past_scores(j):
        k0 = pl.multiple_of(j * MOBA_BLOCK, MOBA_BLOCK)
        out = []
        for g, cols in enumerate(head_cols):
            onehot = jnp.where(lane == g * nb + j, 1.0, 0.0).astype(_BF16)
            k_aug = jnp.concatenate([k_ref[pl.ds(k0, MOBA_BLOCK), cols], onehot], axis=1)
            out.append(jnp.dot(k_aug, qa_ref[g], preferred_element_type=_F32))
        return out

    def compute_scores(j, s_ref):
        for g, s in enumerate(past_scores(j)):
            s_ref[g] = s

    def consume_scores(j, s_ref):
        probs, corrs = [], []
        for g in range(n_heads):
            s = s_ref[g]
            m_old = m_ref[g]
            m_new = jnp.maximum(m_old, jnp.max(s, axis=0, keepdims=True))
            corr = jnp.exp2(m_old - m_new)
            pj = jnp.exp2(s - m_new)
            l_ref[g] = corr * l_ref[g] + jnp.sum(pj, axis=0, keepdims=True)
            m_ref[g] = m_new
            probs.append(pj.astype(_BF16))
            corrs.append(corr)
        for g, cols in enumerate(head_cols):
            acc_ref[g] = corrs[g] * acc_ref[g] + jnp.dot(vt_ref[j, cols, :], probs[g],
                                                         preferred_element_type=_F32)

    last = nb - 1
    compute_scores(0, sa_ref)

    def step(t, carry):
        j = 2 * t
        compute_scores(jnp.minimum(j + 1, last), sb_ref)
        consume_scores(j, sa_ref)
        compute_scores(jnp.minimum(j + 2, last), sa_ref)
        consume_scores(jnp.minimum(j + 1, last), sb_ref)
        return carry

    lax.fori_loop(0, (qi + 1) // 2, step, 0)
    for g in range(n_heads):
        out_t = acc_ref[g] / l_ref[g]
        o_ref[:, g * LANES:(g + 1) * LANES] = out_t.T.astype(o_ref.dtype)


def _attn_prompt(qt, kb, vt, bias_t, head_dim, hpb, nb):
    AW, S = qt.shape
    assert head_dim == LANES and hpb * nb <= LANES
    tq = MOBA_BLOCK
    gw = hpb * head_dim
    return pl.pallas_call(
        functools.partial(_attn_prompt_kernel, nb=nb),
        grid=(AW // gw, S // tq),
        in_specs=[pl.BlockSpec((gw, tq), lambda h, i: (h, i)),
                  pl.BlockSpec((S, gw), lambda h, i: (0, h)),
                  pl.BlockSpec((nb, gw, MOBA_BLOCK), lambda h, i: (0, h, 0)),
                  pl.BlockSpec((hpb * nb, tq), lambda h, i: (h, i))],
        out_specs=pl.BlockSpec((tq, gw), lambda h, i: (i, h)),
        out_shape=jax.ShapeDtypeStruct((S, AW), _BF16),
        scratch_shapes=[pltpu.VMEM((hpb, 2 * LANES, tq), _BF16),
                        pltpu.VMEM((hpb, MOBA_BLOCK, tq), _F32),
                        pltpu.VMEM((hpb, MOBA_BLOCK, tq), _F32),
                        pltpu.VMEM((hpb, 1, tq), _F32), pltpu.VMEM((hpb, 1, tq), _F32),
                        pltpu.VMEM((hpb, head_dim, tq), _F32)],
        compiler_params=_params(2),
        name="moba_prompt",
    )(qt, kb, vt, bias_t)


PAGES_PER_BLOCK = 2


def _attn_sample_kernel(pt_ref, q_ref, kn_ref, vn_ref, *refs, n_blocks, blocks_per_step,
                        scale_log2e):
    del pt_ref
    n_pages = PAGES_PER_BLOCK * blocks_per_step
    k_refs, v_refs = refs[:n_pages], refs[n_pages:2 * n_pages]
    o_ref, m_ref, l_ref, g_ref, acc_ref = refs[2 * n_pages:]
    step = pl.program_id(1)
    page, H, Dh = k_refs[0].shape
    rows = page * H
    q = q_ref[...]
    q_hi = q.astype(_BF16)
    q_lo = (q - q_hi.astype(_F32)).astype(_BF16)
    q2 = jnp.concatenate([q_hi, q_lo], axis=0)

    def scores(k_ref):
        k16 = k_ref[...].reshape(rows, Dh).astype(_BF16)
        s2 = lax.dot_general(q2, k16, _NT, preferred_element_type=_F32)
        return s2[0:H] + s2[H:2 * H]

    head_of_row = lax.broadcasted_iota(jnp.int32, (H, PAGES_PER_BLOCK * rows), 1) % H
    own = head_of_row == lax.broadcasted_iota(jnp.int32, (H, PAGES_PER_BLOCK * rows), 0)
    for t in range(blocks_per_step):
        k0_ref, k1_ref = k_refs[2 * t], k_refs[2 * t + 1]
        v0_ref, v1_ref = v_refs[2 * t], v_refs[2 * t + 1]
        s = jnp.concatenate([scores(k0_ref), scores(k1_ref)], axis=1)
        sm = jnp.where(own, s, -jnp.inf)
        m = jnp.max(sm, axis=1, keepdims=True)
        p = jnp.exp2((sm - m) * scale_log2e)
        l = jnp.sum(p, axis=1, keepdims=True)
        g = jnp.sum(jnp.where(own, s, 0.0), axis=1, keepdims=True)
        p16 = p.astype(_BF16)
        o = (jnp.dot(p16[:, 0:rows], v0_ref[...].reshape(rows, Dh).astype(_BF16),
                     preferred_element_type=_F32)
             + jnp.dot(p16[:, rows:2 * rows], v1_ref[...].reshape(rows, Dh).astype(_BF16),
                       preferred_element_type=_F32))
        n = step * blocks_per_step + t
        m_ref[n] = jnp.broadcast_to(m, (H, LANES))
        l_ref[n] = jnp.broadcast_to(l, (H, LANES))
        g_ref[n] = jnp.broadcast_to(g, (H, LANES))
        acc_ref[n] = o

    @pl.when(step == n_blocks // blocks_per_step - 1)
    def _():
        gates = [g_ref[b] for b in range(n_blocks)]
        s_own = jnp.sum(q * kn_ref[...], axis=1, keepdims=True)
        m_tot = jnp.broadcast_to(s_own, (H, LANES))
        sel = []
        for b in range(n_blocks):
            rank = jnp.zeros((H, LANES), _F32)
            for c in range(n_blocks):
                if c == b:
                    continue
                ahead = gates[c] >= gates[b] if c < b else gates[c] > gates[b]
                rank = rank + jnp.where(ahead, 1.0, 0.0)
            sel.append(rank < float(MOBA_TOPK))
            m_tot = jnp.where(sel[b], jnp.maximum(m_tot, m_ref[b]), m_tot)
        w_own = jnp.exp2((s_own - m_tot) * scale_log2e)
        den = w_own
        num = w_own * vn_ref[...]
        for b in range(n_blocks):
            w = jnp.where(sel[b], jnp.exp2((m_ref[b] - m_tot) * scale_log2e), 0.0)
            den = den + w * l_ref[b]
            num = num + w * acc_ref[b]
        o_ref[...] = num / den


def _attn_sample(q3, kn3, vn3, cache_k, cache_v, page_table, past_len):
    Bd, H, Dh = q3.shape
    page = cache_k.shape[1]
    assert Dh == LANES and MOBA_BLOCK == PAGES_PER_BLOCK * page
    assert past_len % MOBA_BLOCK == 0
    n_blocks = past_len // MOBA_BLOCK
    assert n_blocks >= MOBA_TOPK
    bps = _largest_tile(n_blocks, (4, 2, 1))
    pps = PAGES_PER_BLOCK * bps
    n_pages = page_table.shape[1]
    pt = page_table.reshape(-1)
    tok = pl.BlockSpec((None, H, Dh), lambda b, n, pt: (b, 0, 0))

    def page_spec(which):
        return pl.BlockSpec((None, page, H, Dh),
                            lambda b, n, pt: (pt[b * n_pages + pps * n + which], 0, 0, 0))

    pages = [page_spec(w) for w in range(pps)]
    grid_spec = pltpu.PrefetchScalarGridSpec(
        num_scalar_prefetch=1,
        grid=(Bd, n_blocks // bps),
        in_specs=[tok, tok, tok] + pages + pages,
        out_specs=tok,
        scratch_shapes=[pltpu.VMEM((n_blocks, H, LANES), _F32),
                        pltpu.VMEM((n_blocks, H, LANES), _F32),
                        pltpu.VMEM((n_blocks, H, LANES), _F32),
                        pltpu.VMEM((n_blocks, H, Dh), _F32)],
    )
    return pl.pallas_call(
        functools.partial(_attn_sample_kernel, n_blocks=n_blocks, blocks_per_step=bps,
                          scale_log2e=Dh ** -0.5 * math.log2(math.e)),
        grid_spec=grid_spec,
        out_shape=jax.ShapeDtypeStruct((Bd, H, Dh), _F32),
        compiler_params=_params(2),
        name="moba_sample",
    )(pt, q3, kn3, vn3, *([cache_k] * pps), *([cache_v] * pps))


def _oproj_kernel(ap_ref, as_ref, xp_ref, xs_ref, wo_ref, g_ref, b_ref, op_ref, os_ref,
                  *, npt, alpha):
    i = pl.program_id(0)

    def body(a_ref, x_ref, o_ref):
        m = jnp.dot(a_ref[...].astype(_BF16), wo_ref[...], preferred_element_type=_F32)
        o_ref[...] = _layer_norm(alpha * x_ref[...] + m, g_ref[...], b_ref[...])

    _dual(i, npt, lambda: body(ap_ref, xp_ref, op_ref), lambda: body(as_ref, xs_ref, os_ref))


def _oproj(ap, a_s, xp, xs, wo, g4, b4, layer, alpha, tm):
    S, D = xp.shape
    ns = xs.shape[0]
    AW = ap.shape[1]
    npt = S // tm
    vec = pl.BlockSpec((None, None, 1, D), lambda i: (layer, 1, 0, 0))
    return pl.pallas_call(
        functools.partial(_oproj_kernel, npt=npt, alpha=alpha),
        grid=(npt + 1,),
        in_specs=[pl.BlockSpec((tm, AW), _prompt_rows(npt)),
                  pl.BlockSpec((ns, AW), lambda i: (0, 0)),
                  pl.BlockSpec((tm, D), _prompt_rows(npt)),
                  pl.BlockSpec((ns, D), lambda i: (0, 0)),
                  pl.BlockSpec((None, AW, D), lambda i: (0, 0, 0)), vec, vec],
        out_specs=[pl.BlockSpec((tm, D), _prompt_rows(npt)),
                   pl.BlockSpec((ns, D), lambda i: (0, 0))],
        out_shape=[jax.ShapeDtypeStruct((S, D), _F32), jax.ShapeDtypeStruct((ns, D), _F32)],
        compiler_params=_params(1),
        name="oproj_postnorm",
    )(ap, a_s, xp, xs, wo, g4, b4)


def kernel(x_prompt, x_sample, state_conv, cache_k, cache_v, page_table, ffn_w_gate, ffn_w_up, ffn_w_down, ln_g, ln_b, conv_w_pw1, conv_b_pw1, conv_w_dw, conv_b_dw, conv_ln_g, conv_ln_b, conv_w_pw2, conv_b_pw2, attn_w_q, attn_w_o, w_kv):
    B, S, D = x_prompt.shape
    Bd, Sd, _ = x_sample.shape
    depth = ffn_w_gate.shape[0]
    n_phys, page, H, Dh = cache_k.shape
    past_len = page_table.shape[1] * page
    conv_w = conv_w_dw.shape[1]
    assert B == 1 and Sd == 1 and depth == 2 and conv_w_pw1.shape[0] == 1
    assert S % MOBA_BLOCK == 0 and H * Dh == attn_w_q.shape[-1]
    alpha = (2.0 * depth) ** 0.25
    tm = _largest_tile(S, (512, 256))
    tm_ffn = _largest_tile(S, (1024, 512, 256))

    ffn_f32 = (ffn_w_gate, ffn_w_up, ffn_w_down)
    w16 = tuple(w[0, 0].astype(_BF16) for w in ffn_f32)
    g4, b4 = ln_g[:, :, None, :], ln_b[:, :, None, :]
    half = Dh // 2
    inv = ROPE_THETA ** (-jnp.arange(half, dtype=_F32) / half)
    inv128 = jnp.concatenate([inv, inv])[None, :]

    xp, xs = x_prompt.reshape(S, D), x_sample.reshape(Bd, D)

    xp, xs, w16 = _ffn(xp, xs, w16, g4, b4, 0, 0, alpha, tm_ffn, ffn_f32 + (0, 1))
    up, us = _glu(xp, xs, conv_w_pw1.astype(_BF16), conv_b_pw1, tm_ffn)
    xp, xs, conv_s = _conv_mixer(up, us, state_conv[0], xp, xs, conv_w_dw[0], conv_b_dw,
                                 conv_ln_g, conv_ln_b, conv_w_pw2[0].astype(_BF16), conv_b_pw2,
                                 ln_g[0, 1:2], ln_b[0, 1:2], alpha)
    xp, xs, w16 = _ffn(xp, xs, w16, g4, b4, 0, 1, alpha, tm_ffn, ffn_f32 + (1, 0))
    kp, vp, kbp, vt, kmean, ks, vs = _kv(xp, xs, w_kv.astype(_BF16), inv128, past_len, tm_ffn)

    xp, xs, w16 = _ffn(xp, xs, w16, g4, b4, 1, 0, alpha, tm_ffn, ffn_f32 + (1, 1))
    nb = S // MOBA_BLOCK
    qt, bias_t, qs = _q_proj(xp, xs, attn_w_q.astype(_BF16), inv128, kmean, past_len, tm_ffn)
    hpb = _q_col_tile(H * Dh) // LANES
    ap = _attn_prompt(qt, kbp, vt, bias_t, Dh, hpb, nb)
    a_s = _attn_sample(qs.reshape(Bd, H, Dh), ks.reshape(Bd, H, Dh), vs.reshape(Bd, H, Dh),
                       cache_k, cache_v, page_table, past_len)
    xp, xs = _oproj(ap, a_s.reshape(Bd, H * Dh), xp, xs, attn_w_o.astype(_BF16), g4, b4, 1,
                    alpha, tm)
    xp, xs, _ = _ffn(xp, xs, w16, g4, b4, 1, 1, alpha, tm_ffn)

    conv_p = up[S - (conv_w - 1):].reshape(1, 1, conv_w - 1, D)
    return (xp.reshape(1, S, D), xs.reshape(Bd, 1, D), conv_p, conv_s[None],
            kp.reshape(1, S, H, Dh), vp.reshape(1, S, H, Dh),
            ks.reshape(Bd, 1, H, Dh), vs.reshape(Bd, 1, H, Dh))
```

```python
import functools
import math

import jax
import jax.numpy as jnp
from jax import lax
from jax.experimental import pallas as pl
from jax.experimental.pallas import tpu as pltpu

_F32 = jnp.float32
_BF16 = jnp.bfloat16

MOBA_BLOCK = 256
MOBA_TOPK = 3
ROPE_THETA = 10000.0
LN_EPS = 1e-5
LANES = 128
MASK_BIAS = -1e30
V7X_VMEM_LIMIT = 60 * 1024 * 1024

_NT = (((1,), (1,)), ((), ()))


def _params(n_grid_dims, vmem_bytes=V7X_VMEM_LIMIT):
    return pltpu.CompilerParams(
        dimension_semantics=("arbitrary",) * n_grid_dims, vmem_limit_bytes=vmem_bytes)


def _largest_tile(n, candidates):
    for c in candidates:
        if n % c == 0:
            return c
    return n


def _layer_norm(z, g, b):
    mu = jnp.mean(z, axis=-1, keepdims=True)
    zc = z - mu
    var = jnp.mean(zc * zc, axis=-1, keepdims=True)
    return zc * lax.rsqrt(var + LN_EPS) * g + b


def _sigmoid(a):
    return 1.0 / (1.0 + jnp.exp(-a))


def _dual(i, n_prompt_tiles, prompt_fn, sample_fn):
    pl.when(i < n_prompt_tiles)(prompt_fn)
    pl.when(i == n_prompt_tiles)(sample_fn)


def _prompt_rows(npt):
    return lambda i, *_: (jnp.minimum(i, npt - 1), 0)


def _prompt_tile(npt, nj):
    return lambda i, j: (jnp.minimum(i, npt - 1), jnp.where(i < npt, j, nj - 1))


def _sample_tile(npt):
    return lambda i, j: (0, jnp.where(i < npt, 0, j))


def _ffn_kernel(xp_ref, xs_ref, wg_ref, wu_ref, wd_ref, g_ref, b_ref, *refs, npt, ns, alpha,
                cast_cols, n_col_tiles):
    n_cast = len(cast_cols)
    cast_in, refs = refs[:n_cast], refs[n_cast:]
    op_ref, os_ref = refs[:2]
    cast_out, xb_ref = refs[2:2 + n_cast], refs[2 + n_cast]
    i, j, nj = pl.program_id(0), pl.program_id(1), pl.num_programs(1)

    def body(x_ref, o_ref, xb):
        @pl.when(j == 0)
        def _():
            xb[...] = x_ref[...].astype(_BF16)
            o_ref[...] = jnp.zeros_like(o_ref)

        x16 = xb[...]
        a = jnp.dot(x16, wg_ref[...], preferred_element_type=_F32)
        u = jnp.dot(x16, wu_ref[...], preferred_element_type=_F32)
        h = (a * _sigmoid(a)) * u
        o_ref[...] += jnp.dot(h.astype(_BF16), wd_ref[...], preferred_element_type=_F32)

        @pl.when(j == nj - 1)
        def _():
            z = alpha * x_ref[...] + 0.5 * o_ref[...]
            o_ref[...] = _layer_norm(z, g_ref[...], b_ref[...])

    def prompt_step():
        body(xp_ref, op_ref, xb_ref)
        for src, dst, n_cols in zip(cast_in, cast_out, cast_cols):
            def cast(src=src, dst=dst):
                dst[...] = src[...].astype(_BF16)
            if n_cols == n_col_tiles:
                cast()
            else:
                pl.when(j < n_cols)(cast)

    _dual(i, npt, prompt_step, lambda: body(xs_ref, os_ref, xb_ref.at[pl.ds(0, ns)]))


def _ffn(xp, xs, w16, g4, b4, layer, which, alpha, tm, next_f32=None, side_f32=()):
    norm = 2 * which
    wg, wu, wd = w16
    S, D = xp.shape
    ns = xs.shape[0]
    F = wg.shape[-1]
    tf = _largest_tile(F, (512, 256, 128))
    npt, nj = S // tm, F // tf
    w_in = pl.BlockSpec((D, tf), lambda i, j: (0, j))
    w_out = pl.BlockSpec((tf, D), lambda i, j: (j, 0))
    vec = pl.BlockSpec((None, None, 1, D), lambda i, j: (layer, norm, 0, 0))
    single = pl.Buffered(1)
    cast_args, cast_in, cast_out, cast_shapes, cast_cols = [], [], [], [], []
    tile = _prompt_tile(npt, nj)
    if next_f32 is not None:
        ng, nu, nd, nl, nw = next_f32
        rows = D // npt
        assert rows % LANES == 0
        cast_args = [ng, nu, nd]
        cast_cols = [nj, nj, nj]
        in_map = lambda i, j: (nl, nw) + tile(i, j)
        in_map_t = lambda i, j: (nl, nw) + tile(i, j)[::-1]
        cast_in = [pl.BlockSpec((None, None, rows, tf), in_map),
                   pl.BlockSpec((None, None, rows, tf), in_map),
                   pl.BlockSpec((None, None, tf, rows), in_map_t)]
        cast_out = [pl.BlockSpec((rows, tf), tile), pl.BlockSpec((rows, tf), tile),
                    pl.BlockSpec((tf, rows), lambda i, j: tile(i, j)[::-1])]
        cast_shapes = [jax.ShapeDtypeStruct((D, F), _BF16), jax.ShapeDtypeStruct((D, F), _BF16),
                       jax.ShapeDtypeStruct((F, D), _BF16)]
    n_next = len(cast_args)
    for w in side_f32:
        k_dim, n_dim = w.shape
        rows, n_cols = k_dim // npt, n_dim // tf
        assert k_dim % npt == 0 and rows % 16 == 0 and n_dim % tf == 0 and n_cols <= nj

        def side_map(i, j, n_cols=n_cols):
            r, c = tile(i, j)
            return r, jnp.minimum(c, n_cols - 1)

        cast_args.append(w)
        cast_cols.append(n_cols)
        cast_in.append(pl.BlockSpec((rows, tf), side_map))
        cast_out.append(pl.BlockSpec((rows, tf), side_map))
        cast_shapes.append(jax.ShapeDtypeStruct((k_dim, n_dim), _BF16))
    outs = pl.pallas_call(
        functools.partial(_ffn_kernel, npt=npt, ns=ns, alpha=alpha, cast_cols=tuple(cast_cols),
                          n_col_tiles=nj),
        grid=(npt + 1, nj),
        in_specs=[pl.BlockSpec((tm, D), _prompt_rows(npt), pipeline_mode=single),
                  pl.BlockSpec((ns, D), lambda i, j: (0, 0), pipeline_mode=single),
                  w_in, w_in, w_out, vec, vec] + cast_in,
        out_specs=[pl.BlockSpec((tm, D), _prompt_rows(npt), pipeline_mode=single),
                   pl.BlockSpec((ns, D), lambda i, j: (0, 0), pipeline_mode=single)] + cast_out,
        out_shape=[jax.ShapeDtypeStruct((S, D), _F32),
                   jax.ShapeDtypeStruct((ns, D), _F32)] + cast_shapes,
        scratch_shapes=[pltpu.VMEM((tm, D), _BF16)],
        compiler_params=_params(2),
        name="ffn_postnorm",
    )(xp, xs, wg, wu, wd, g4, b4, *cast_args)
    return outs[0], outs[1], tuple(outs[2:2 + n_next]), tuple(outs[2 + n_next:])


def _glu_kernel(xp_ref, xs_ref, wa_ref, wb_ref, ba_ref, bb_ref, op_ref, os_ref, xb_ref,
                *, npt, ns):
    i, j = pl.program_id(0), pl.program_id(1)

    def body(x_ref, o_ref, xb):
        @pl.when(j == 0)
        def _():
            xb[...] = x_ref[...].astype(_BF16)

        x16 = xb[...]
        a = jnp.dot(x16, wa_ref[...], preferred_element_type=_F32) + ba_ref[...]
        gate = jnp.dot(x16, wb_ref[...], preferred_element_type=_F32) + bb_ref[...]
        o_ref[...] = a * _sigmoid(gate)

    _dual(i, npt, lambda: body(xp_ref, op_ref, xb_ref),
          lambda: body(xs_ref, os_ref, xb_ref.at[pl.ds(0, ns)]))


def _glu(xp, xs, w_pw1, b_pw1, tm):
    S, D = xp.shape
    ns = xs.shape[0]
    tn = _largest_tile(D, (512, 256, 128))
    npt, nj = S // tm, D // tn
    b3 = b_pw1.reshape(1, 1, 2 * D)
    return pl.pallas_call(
        functools.partial(_glu_kernel, npt=npt, ns=ns),
        grid=(npt + 1, nj),
        in_specs=[pl.BlockSpec((tm, D), _prompt_rows(npt)),
                  pl.BlockSpec((ns, D), lambda i, j: (0, 0)),
                  pl.BlockSpec((D, tn), lambda i, j: (0, j)),
                  pl.BlockSpec((D, tn), lambda i, j: (0, j + nj)),
                  pl.BlockSpec((None, 1, tn), lambda i, j: (0, 0, j)),
                  pl.BlockSpec((None, 1, tn), lambda i, j: (0, 0, j + nj))],
        out_specs=[pl.BlockSpec((tm, tn), _prompt_tile(npt, nj)),
                   pl.BlockSpec((ns, tn), _sample_tile(npt))],
        out_shape=[jax.ShapeDtypeStruct((S, D), _F32), jax.ShapeDtypeStruct((ns, D), _F32)],
        scratch_shapes=[pltpu.VMEM((tm, D), _BF16)],
        compiler_params=_params(2),
        name="pw1_glu",
    )(xp, xs, w_pw1, w_pw1, b3, b3)


def _conv_tail(y, x1, gcn_ref, bcn_ref, w2_ref, b2_ref, g_ref, b_ref, alpha):
    z = _layer_norm(y, gcn_ref[...], bcn_ref[...])
    z = z * _sigmoid(z)
    m = jnp.dot(z.astype(_BF16), w2_ref[...], preferred_element_type=_F32) + b2_ref[...]
    return _layer_norm(alpha * x1 + m, g_ref[...], b_ref[...])


CONV_HALO = 32
CONV_ROWS = 64
CONV_COLS = 256
SUBLANES = 8


def _conv_prompt_kernel(u_ref, halo_ref, x1_ref, wdw_ref, bdw_ref, gcn_ref, bcn_ref, w2_ref,
                        b2_ref, g_ref, b_ref, o_ref, ext_ref, y_ref, wb_ref, *, conv_w, alpha):
    i = pl.program_id(0)
    tm, D = u_ref.shape

    @pl.when(i == 0)
    def _():
        for k in range(conv_w):
            wb_ref[k] = jnp.broadcast_to(wdw_ref[k:k + 1, :], (SUBLANES, D))

    ext_ref[0:CONV_HALO, :] = jnp.where(i == 0, 0.0, halo_ref[...])
    ext_ref[CONV_HALO:, :] = u_ref[...]
    lead = CONV_HALO - (conv_w - 1)
    cc = min(CONV_COLS, D)

    def row_chunk(r, carry):
        r0 = pl.multiple_of(r * CONV_ROWS, CONV_ROWS)
        for c in range(D // cc):
            cols = slice(c * cc, (c + 1) * cc)
            base = ext_ref[pl.ds(r0, CONV_ROWS + CONV_HALO), cols]
            acc = jnp.zeros((CONV_ROWS // SUBLANES, SUBLANES, cc), _F32)
            for shift in range(SUBLANES):
                offs = [o for o in range(shift, lead + conv_w, SUBLANES) if o >= lead]
                if not offs:
                    continue
                win = base[offs[0]:offs[-1] + CONV_ROWS]
                for o in offs:
                    tap = win[o - offs[0]:o - offs[0] + CONV_ROWS]
                    tap = tap.reshape(CONV_ROWS // SUBLANES, SUBLANES, cc)
                    acc = acc + wb_ref[o - lead, :, cols][None] * tap
            y_ref[pl.ds(r0, CONV_ROWS), cols] = acc.reshape(CONV_ROWS, cc) + bdw_ref[:, cols]
        return carry

    lax.fori_loop(0, tm // CONV_ROWS, row_chunk, 0)
    o_ref[...] = _conv_tail(y_ref[...], x1_ref[...], gcn_ref, bcn_ref, w2_ref, b2_ref,
                            g_ref, b_ref, alpha)


def _conv_sample_kernel(st_ref, u_ref, x1_ref, wdw_ref, bdw_ref, gcn_ref, bcn_ref, w2_ref,
                        b2_ref, g_ref, b_ref, o_ref, ost_ref, *, conv_w, alpha):
    w = wdw_ref[...]
    u = u_ref[...]
    y = jnp.sum(st_ref[...] * w[None, 0:conv_w - 1, :], axis=1)
    y = y + u * w[conv_w - 1:conv_w, :] + bdw_ref[...]
    o_ref[...] = _conv_tail(y, x1_ref[...], gcn_ref, bcn_ref, w2_ref, b2_ref, g_ref, b_ref,
                            alpha)
    ost_ref[:, 0:conv_w - 2, :] = st_ref[:, 1:conv_w - 1, :]
    ost_ref[:, conv_w - 2:conv_w - 1, :] = u[:, None, :]


def _conv_mixer(up, us, state, x1p, x1s, wdw, bdw, gcn, bcn, w2, b2, g, b, alpha):
    S, D = up.shape
    ns = us.shape[0]
    conv_w = wdw.shape[0]
    assert conv_w - 1 <= CONV_HALO
    tm = _largest_tile(S, (256, 128))
    per_halo = tm // CONV_HALO
    full = lambda shape: pl.BlockSpec(shape, lambda i: (0,) * len(shape))
    shared = [full((conv_w, D)), full((1, D)), full((1, D)), full((1, D)), full((D, D)),
              full((1, D)), full((1, D)), full((1, D))]
    shared_args = (wdw, bdw, gcn, bcn, w2, b2, g, b)
    xp = pl.pallas_call(
        functools.partial(_conv_prompt_kernel, conv_w=conv_w, alpha=alpha),
        grid=(S // tm,),
        in_specs=[pl.BlockSpec((tm, D), lambda i: (i, 0)),
                  pl.BlockSpec((CONV_HALO, D), lambda i: (jnp.maximum(i * per_halo - 1, 0), 0)),
                  pl.BlockSpec((tm, D), lambda i: (i, 0))] + shared,
        out_specs=pl.BlockSpec((tm, D), lambda i: (i, 0)),
        out_shape=jax.ShapeDtypeStruct((S, D), _F32),
        scratch_shapes=[pltpu.VMEM((CONV_HALO + tm, D), _F32), pltpu.VMEM((tm, D), _F32),
                        pltpu.VMEM((conv_w, SUBLANES, D), _F32)],
        compiler_params=_params(1),
        name="conv_prompt",
    )(up, up, x1p, *shared_args)
    tb = _largest_tile(ns, (16, 8))
    state_spec = pl.BlockSpec((tb, conv_w - 1, D), lambda i: (i, 0, 0))
    xs, new_state = pl.pallas_call(
        functools.partial(_conv_sample_kernel, conv_w=conv_w, alpha=alpha),
        grid=(ns // tb,),
        in_specs=[state_spec,
                  pl.BlockSpec((tb, D), lambda i: (i, 0)),
                  pl.BlockSpec((tb, D), lambda i: (i, 0))] + shared,
        out_specs=[pl.BlockSpec((tb, D), lambda i: (i, 0)), state_spec],
        out_shape=[jax.ShapeDtypeStruct((ns, D), _F32),
                   jax.ShapeDtypeStruct((ns, conv_w - 1, D), _F32)],
        compiler_params=_params(1),
        name="conv_sample",
    )(state, us, x1s, *shared_args)
    return xp, xs, new_state


def _rope_tables(pos, inv_ref, cos_ref, sin_ref):
    ang = pos * inv_ref[...]
    lane = lax.broadcasted_iota(jnp.int32, ang.shape, 1)
    cos_ref[...] = jnp.cos(ang)
    sin_ref[...] = jnp.where(lane < LANES // 2, -1.0, 1.0) * jnp.sin(ang)


def _rope_head(x, cos, sin):
    return x * cos + pltpu.roll(x, LANES // 2, axis=1) * sin


def _row_positions(i, tm):
    return (i * tm + lax.broadcasted_iota(jnp.int32, (tm, LANES), 0)).astype(_F32)


def _kv_kernel(xp_ref, xs_ref, wk_ref, wv_ref, inv_ref,
               kp_ref, vp_ref, kbp_ref, vbp_ref, km_ref, ks_ref, vs_ref,
               xb_ref, cos_ref, sin_ref, *, npt, ns, past_len):
    i, j = pl.program_id(0), pl.program_id(1)
    tn = wk_ref.shape[1]

    def body(x_ref, xb, cos_r, sin_r, pos_fn, k_ref, v_ref, kb_ref, vb_ref, kmean_ref):
        rows = x_ref.shape[0]

        @pl.when(j == 0)
        def _():
            xb[...] = x_ref[...].astype(_BF16)
            _rope_tables(pos_fn(rows), inv_ref, cos_r, sin_r)

        x16 = xb[...]
        k = jnp.dot(x16, wk_ref[...], preferred_element_type=_F32)
        v = jnp.dot(x16, wv_ref[...], preferred_element_type=_F32)
        cos, sin = cos_r[...], sin_r[...]
        k = jnp.concatenate(
            [_rope_head(k[:, h * LANES:(h + 1) * LANES], cos, sin) for h in range(tn // LANES)],
            axis=1)
        k_ref[...] = k
        v_ref[...] = v
        if kb_ref is not None:
            kb_ref[...] = k.astype(_BF16)
            nblk = rows // MOBA_BLOCK
            for b in range(nblk):
                vb_ref[b] = v[b * MOBA_BLOCK:(b + 1) * MOBA_BLOCK].T.astype(_BF16)
            means = [jnp.mean(k[b * MOBA_BLOCK:(b + 1) * MOBA_BLOCK], axis=0, keepdims=True)
                     for b in range(nblk)]
            means.append(jnp.zeros((kmean_ref.shape[0] - nblk, tn), _F32))
            kmean_ref[...] = jnp.concatenate(means, axis=0)

    _dual(i, npt,
          lambda: body(xp_ref, xb_ref, cos_ref, sin_ref, lambda rows: _row_positions(i, rows),
                       kp_ref, vp_ref, kbp_ref, vbp_ref, km_ref),
          lambda: body(xs_ref, xb_ref.at[pl.ds(0, ns)], cos_ref.at[pl.ds(0, ns)],
                       sin_ref.at[pl.ds(0, ns)],
                       lambda rows: jnp.full((rows, LANES), float(past_len), _F32),
                       ks_ref, vs_ref, None, None, None))


KMEAN_ROWS = 8


def _kv(xp, xs, w_kv, inv128, past_len, tm):
    S, D = xp.shape
    ns = xs.shape[0]
    AW = w_kv.shape[1] // 2
    tn = _largest_tile(AW, (512, 256, 128))
    npt, nj = S // tm, AW // tn
    assert tm % MOBA_BLOCK == 0 and tm // MOBA_BLOCK <= KMEAN_ROWS
    prow, srow = _prompt_tile(npt, nj), _sample_tile(npt)
    outs = pl.pallas_call(
        functools.partial(_kv_kernel, npt=npt, ns=ns, past_len=past_len),
        grid=(npt + 1, nj),
        in_specs=[pl.BlockSpec((tm, D), _prompt_rows(npt)),
                  pl.BlockSpec((ns, D), lambda i, j: (0, 0)),
                  pl.BlockSpec((D, tn), lambda i, j: (0, j)),
                  pl.BlockSpec((D, tn), lambda i, j: (0, j + nj)),
                  pl.BlockSpec((1, LANES), lambda i, j: (0, 0))],
        out_specs=[pl.BlockSpec((tm, tn), prow), pl.BlockSpec((tm, tn), prow),
                   pl.BlockSpec((tm, tn), prow),
                   pl.BlockSpec((tm // MOBA_BLOCK, tn, MOBA_BLOCK),
                                lambda i, j: (prow(i, j)[0], prow(i, j)[1], 0)),
                   pl.BlockSpec((None, KMEAN_ROWS, tn),
                                lambda i, j: (prow(i, j)[0], 0, prow(i, j)[1])),
                   pl.BlockSpec((ns, tn), srow), pl.BlockSpec((ns, tn), srow)],
        out_shape=[jax.ShapeDtypeStruct((S, AW), _F32), jax.ShapeDtypeStruct((S, AW), _F32),
                   jax.ShapeDtypeStruct((S, AW), _BF16),
                   jax.ShapeDtypeStruct((S // MOBA_BLOCK, AW, MOBA_BLOCK), _BF16),
                   jax.ShapeDtypeStruct((npt, KMEAN_ROWS, AW), _F32),
                   jax.ShapeDtypeStruct((ns, AW), _F32), jax.ShapeDtypeStruct((ns, AW), _F32)],
        scratch_shapes=[pltpu.VMEM((tm, D), _BF16), pltpu.VMEM((tm, LANES), _F32),
                        pltpu.VMEM((tm, LANES), _F32)],
        compiler_params=_params(2),
        name="kv_rope",
    )(xp, xs, w_kv, w_kv, inv128)
    kp, vp, kbp, vtp, km3, ks, vs = outs
    kmean = km3[:, :tm // MOBA_BLOCK].reshape(S // MOBA_BLOCK, AW)
    return kp, vp, kbp, vtp, kmean, ks, vs


def _select_bias(gate_t, query_block):
    nb = gate_t.shape[0]
    blk = lax.broadcasted_iota(jnp.int32, gate_t.shape, 0).astype(_F32)
    g = jnp.where(blk < query_block, gate_t, -jnp.inf)
    sel = jnp.zeros(gate_t.shape, jnp.bool_)
    for _ in range(MOBA_TOPK):
        mx = jnp.max(g, axis=0, keepdims=True)
        is_max = jnp.logical_and(g == mx, g > -jnp.inf)
        first = jnp.min(jnp.where(is_max, blk, float(nb)), axis=0, keepdims=True)
        pick = blk == first
        sel = jnp.logical_or(sel, pick)
        g = jnp.where(pick, -jnp.inf, g)
    return jnp.where(sel, 0.0, MASK_BIAS)


def _q_kernel(xp_ref, xs_ref, wq_ref, inv_ref, km_ref, qt_ref, bias_ref, qs_ref,
              xb_ref, cos_ref, sin_ref, *, npt, ns, past_len, score_scale):
    i, j = pl.program_id(0), pl.program_id(1)
    tn = wq_ref.shape[1]
    nb = km_ref.shape[0]

    def body(x_ref, xb, cos_r, sin_r, pos_fn, prompt):
        rows = x_ref.shape[0]

        @pl.when(j == 0)
        def _():
            xb[...] = x_ref[...].astype(_BF16)
            _rope_tables(pos_fn(rows), inv_ref, cos_r, sin_r)

        q = jnp.dot(xb[...], wq_ref[...], preferred_element_type=_F32)
        cos, sin = cos_r[...], sin_r[...]
        heads = [_rope_head(q[:, h * LANES:(h + 1) * LANES], cos, sin)
                 for h in range(tn // LANES)]
        q = jnp.concatenate(heads, axis=1)
        if not prompt:
            qs_ref[...] = q
            return
        qt_ref[...] = (q * score_scale).T.astype(_BF16)
        query_block = (i * rows + lax.broadcasted_iota(jnp.int32, (1, rows), 1)) // MOBA_BLOCK
        query_block = query_block.astype(_F32)
        for h, qh in enumerate(heads):
            gate_t = lax.dot_general(km_ref[:, h * LANES:(h + 1) * LANES], qh, _NT,
                                     preferred_element_type=_F32,
                                     precision=lax.Precision.HIGHEST)
            bias_ref[h * nb:(h + 1) * nb, :] = _select_bias(gate_t, query_block)

    _dual(i, npt,
          lambda: body(xp_ref, xb_ref, cos_ref, sin_ref, lambda rows: _row_positions(i, rows),
                       True),
          lambda: body(xs_ref, xb_ref.at[pl.ds(0, ns)], cos_ref.at[pl.ds(0, ns)],
                       sin_ref.at[pl.ds(0, ns)],
                       lambda rows: jnp.full((rows, LANES), float(past_len), _F32), False))


def _q_col_tile(attn_width):
    return _largest_tile(attn_width, (512, 256, 128))


def _q_proj(xp, xs, wq, inv128, kmean, past_len, tm):
    S, D = xp.shape
    ns = xs.shape[0]
    AW = wq.shape[-1]
    nb = kmean.shape[0]
    tn = _q_col_tile(AW)
    hpb = tn // LANES
    npt, nj = S // tm, AW // tn
    assert nj == 1 or (hpb * nb) % SUBLANES == 0
    prow = _prompt_tile(npt, nj)
    pcol = lambda i, j: prow(i, j)[::-1]
    return pl.pallas_call(
        functools.partial(_q_kernel, npt=npt, ns=ns, past_len=past_len,
                          score_scale=LANES ** -0.5 * math.log2(math.e)),
        grid=(npt + 1, nj),
        in_specs=[pl.BlockSpec((tm, D), _prompt_rows(npt)),
                  pl.BlockSpec((ns, D), lambda i, j: (0, 0)),
                  pl.BlockSpec((D, tn), lambda i, j: (0, j)),
                  pl.BlockSpec((1, LANES), lambda i, j: (0, 0)),
                  pl.BlockSpec((nb, tn), lambda i, j: (0, j))],
        out_specs=[pl.BlockSpec((tn, tm), pcol),
                   pl.BlockSpec((hpb * nb, tm), pcol),
                   pl.BlockSpec((ns, tn), _sample_tile(npt))],
        out_shape=[jax.ShapeDtypeStruct((AW, S), _BF16),
                   jax.ShapeDtypeStruct(((AW // LANES) * nb, S), _F32),
                   jax.ShapeDtypeStruct((ns, AW), _F32)],
        scratch_shapes=[pltpu.VMEM((tm, D), _BF16), pltpu.VMEM((tm, LANES), _F32),
                        pltpu.VMEM((tm, LANES), _F32)],
        compiler_params=_params(2),
        name="q_rope_select",
    )(xp, xs, wq, inv128, kmean)


def _attn_prompt_kernel(qt_ref, k_ref, vt_ref, bias_ref, o_ref, qa_ref, sa_ref, sb_ref, m_ref,
                        l_ref, acc_ref, *, nb):
    qi = pl.program_id(1)
    tq = qt_ref.shape[1]
    n_heads = qt_ref.shape[0] // LANES

    bias = bias_ref[...].astype(_BF16)
    if bias.shape[0] < LANES:
        bias = jnp.concatenate(
            [bias, jnp.zeros((LANES - bias.shape[0], tq), _BF16)], axis=0)
    key_i = lax.broadcasted_iota(jnp.int32, (MOBA_BLOCK, tq), 0)
    qry_i = lax.broadcasted_iota(jnp.int32, (MOBA_BLOCK, tq), 1)
    j0 = pl.multiple_of(qi * MOBA_BLOCK, MOBA_BLOCK)
    head_cols = [slice(g * LANES, (g + 1) * LANES) for g in range(n_heads)]
    scores = [jnp.dot(k_ref[pl.ds(j0, MOBA_BLOCK), cols], qt_ref[cols, :],
                      preferred_element_type=_F32) for cols in head_cols]
    probs = []
    for g in range(n_heads):
        s = jnp.where(key_i <= qry_i, scores[g], -jnp.inf)
        m = jnp.max(s, axis=0, keepdims=True)
        p = jnp.exp2(s - m)
        m_ref[g] = m
        l_ref[g] = jnp.sum(p, axis=0, keepdims=True)
        probs.append(p.astype(_BF16))
    for g, cols in enumerate(head_cols):
        acc_ref[g] = jnp.dot(vt_ref[qi, cols, :], probs[g], preferred_element_type=_F32)
        qa_ref[g] = jnp.concatenate([qt_ref[cols, :], bias], axis=0)

    lane = lax.broadcasted_iota(jnp.int32, (MOBA_BLOCK, LANES), 1)

    def past_scores(j):
        k0 = pl.multiple_of(j * MOBA_BLOCK, MOBA_BLOCK)
        out = []
        for g, cols in enumerate(head_cols):
            onehot = jnp.where(lane == g * nb + j, 1.0, 0.0).astype(_BF16)
            k_aug = jnp.concatenate([k_ref[pl.ds(k0, MOBA_BLOCK), cols], onehot], axis=1)
            out.append(jnp.dot(k_aug, qa_ref[g], preferred_element_type=_F32))
        return out

    def compute_scores(j, s_ref):
        for g, s in enumerate(past_scores(j)):
            s_ref[g] = s

    def consume_scores(j, s_ref):
        probs, corrs = [], []
        for g in range(n_heads):
            s = s_ref[g]
            m_old = m_ref[g]
            m_new = jnp.maximum(m_old, jnp.max(s, axis=0, keepdims=True))
            corr = jnp.exp2(m_old - m_new)
            pj = jnp.exp2(s - m_new)
            l_ref[g] = corr * l_ref[g] + jnp.sum(pj, axis=0, keepdims=True)
            m_ref[g] = m_new
            probs.append(pj.astype(_BF16))
            corrs.append(corr)
        for g, cols in enumerate(head_cols):
            acc_ref[g] = corrs[g] * acc_ref[g] + jnp.dot(vt_ref[j, cols, :], probs[g],
                                                         preferred_element_type=_F32)

    last = nb - 1
    compute_scores(0, sa_ref)

    def step(t, carry):
        j = 2 * t
        compute_scores(jnp.minimum(j + 1, last), sb_ref)
        consume_scores(j, sa_ref)
        compute_scores(jnp.minimum(j + 2, last), sa_ref)
        consume_scores(jnp.minimum(j + 1, last), sb_ref)
        return carry

    lax.fori_loop(0, (qi + 1) // 2, step, 0)
    for g in range(n_heads):
        out_t = acc_ref[g] / l_ref[g]
        o_ref[:, g * LANES:(g + 1) * LANES] = out_t.T.astype(o_ref.dtype)


def _attn_prompt(qt, kb, vt, bias_t, head_dim, hpb, nb):
    AW, S = qt.shape
    assert head_dim == LANES and hpb * nb <= LANES
    tq = MOBA_BLOCK
    gw = hpb * head_dim
    return pl.pallas_call(
        functools.partial(_attn_prompt_kernel, nb=nb),
        grid=(AW // gw, S // tq),
        in_specs=[pl.BlockSpec((gw, tq), lambda h, i: (h, i)),
                  pl.BlockSpec((S, gw), lambda h, i: (0, h)),
                  pl.BlockSpec((nb, gw, MOBA_BLOCK), lambda h, i: (0, h, 0)),
                  pl.BlockSpec((hpb * nb, tq), lambda h, i: (h, i))],
        out_specs=pl.BlockSpec((tq, gw), lambda h, i: (i, h)),
        out_shape=jax.ShapeDtypeStruct((S, AW), _BF16),
        scratch_shapes=[pltpu.VMEM((hpb, 2 * LANES, tq), _BF16),
                        pltpu.VMEM((hpb, MOBA_BLOCK, tq), _F32),
                        pltpu.VMEM((hpb, MOBA_BLOCK, tq), _F32),
                        pltpu.VMEM((hpb, 1, tq), _F32), pltpu.VMEM((hpb, 1, tq), _F32),
                        pltpu.VMEM((hpb, head_dim, tq), _F32)],
        compiler_params=_params(2),
        name="moba_prompt",
    )(qt, kb, vt, bias_t)


PAGES_PER_BLOCK = 2


def _attn_sample_kernel(pt_ref, q_ref, kn_ref, vn_ref, *refs, n_blocks, blocks_per_step,
                        scale_log2e):
    del pt_ref
    n_pages = PAGES_PER_BLOCK * blocks_per_step
    k_refs, v_refs = refs[:n_pages], refs[n_pages:2 * n_pages]
    o_ref, m_ref, l_ref, g_ref, acc_ref = refs[2 * n_pages:]
    step = pl.program_id(1)
    page, H, Dh = k_refs[0].shape
    rows = page * H
    q = q_ref[...]
    q_hi = q.astype(_BF16)
    q_lo = (q - q_hi.astype(_F32)).astype(_BF16)
    q2 = jnp.concatenate([q_hi, q_lo], axis=0)

    def scores(k_ref):
        k16 = k_ref[...].reshape(rows, Dh).astype(_BF16)
        s2 = lax.dot_general(q2, k16, _NT, preferred_element_type=_F32)
        return s2[0:H] + s2[H:2 * H]

    head_of_row = lax.broadcasted_iota(jnp.int32, (H, PAGES_PER_BLOCK * rows), 1) % H
    own = head_of_row == lax.broadcasted_iota(jnp.int32, (H, PAGES_PER_BLOCK * rows), 0)
    for t in range(blocks_per_step):
        k0_ref, k1_ref = k_refs[2 * t], k_refs[2 * t + 1]
        v0_ref, v1_ref = v_refs[2 * t], v_refs[2 * t + 1]
        s = jnp.concatenate([scores(k0_ref), scores(k1_ref)], axis=1)
        sm = jnp.where(own, s, -jnp.inf)
        m = jnp.max(sm, axis=1, keepdims=True)
        p = jnp.exp2((sm - m) * scale_log2e)
        l = jnp.sum(p, axis=1, keepdims=True)
        g = jnp.sum(jnp.where(own, s, 0.0), axis=1, keepdims=True)
        p16 = p.astype(_BF16)
        o = (jnp.dot(p16[:, 0:rows], v0_ref[...].reshape(rows, Dh).astype(_BF16),
                     preferred_element_type=_F32)
             + jnp.dot(p16[:, rows:2 * rows], v1_ref[...].reshape(rows, Dh).astype(_BF16),
                       preferred_element_type=_F32))
        n = step * blocks_per_step + t
        m_ref[n] = jnp.broadcast_to(m, (H, LANES))
        l_ref[n] = jnp.broadcast_to(l, (H, LANES))
        g_ref[n] = jnp.broadcast_to(g, (H, LANES))
        acc_ref[n] = o

    @pl.when(step == n_blocks // blocks_per_step - 1)
    def _():
        gates = [g_ref[b] for b in range(n_blocks)]
        s_own = jnp.sum(q * kn_ref[...], axis=1, keepdims=True)
        m_tot = jnp.broadcast_to(s_own, (H, LANES))
        sel = []
        for b in range(n_blocks):
            rank = jnp.zeros((H, LANES), _F32)
            for c in range(n_blocks):
                if c == b:
                    continue
                ahead = gates[c] >= gates[b] if c < b else gates[c] > gates[b]
                rank = rank + jnp.where(ahead, 1.0, 0.0)
            sel.append(rank < float(MOBA_TOPK))
            m_tot = jnp.where(sel[b], jnp.maximum(m_tot, m_ref[b]), m_tot)
        w_own = jnp.exp2((s_own - m_tot) * scale_log2e)
        den = w_own
        num = w_own * vn_ref[...]
        for b in range(n_blocks):
            w = jnp.where(sel[b], jnp.exp2((m_ref[b] - m_tot) * scale_log2e), 0.0)
            den = den + w * l_ref[b]
            num = num + w * acc_ref[b]
        o_ref[...] = num / den


def _attn_sample(q3, kn3, vn3, cache_k, cache_v, page_table, past_len):
    Bd, H, Dh = q3.shape
    page = cache_k.shape[1]
    assert Dh == LANES and MOBA_BLOCK == PAGES_PER_BLOCK * page
    assert past_len % MOBA_BLOCK == 0
    n_blocks = past_len // MOBA_BLOCK
    assert n_blocks >= MOBA_TOPK
    bps = _largest_tile(n_blocks, (4, 2, 1))
    pps = PAGES_PER_BLOCK * bps
    n_pages = page_table.shape[1]
    pt = page_table.reshape(-1)
    tok = pl.BlockSpec((None, H, Dh), lambda b, n, pt: (b, 0, 0))

    def page_spec(which):
        return pl.BlockSpec((None, page, H, Dh),
                            lambda b, n, pt: (pt[b * n_pages + pps * n + which], 0, 0, 0))

    pages = [page_spec(w) for w in range(pps)]
    grid_spec = pltpu.PrefetchScalarGridSpec(
        num_scalar_prefetch=1,
        grid=(Bd, n_blocks // bps),
        in_specs=[tok, tok, tok] + pages + pages,
        out_specs=tok,
        scratch_shapes=[pltpu.VMEM((n_blocks, H, LANES), _F32),
                        pltpu.VMEM((n_blocks, H, LANES), _F32),
                        pltpu.VMEM((n_blocks, H, LANES), _F32),
                        pltpu.VMEM((n_blocks, H, Dh), _F32)],
    )
    return pl.pallas_call(
        functools.partial(_attn_sample_kernel, n_blocks=n_blocks, blocks_per_step=bps,
                          scale_log2e=Dh ** -0.5 * math.log2(math.e)),
        grid_spec=grid_spec,
        out_shape=jax.ShapeDtypeStruct((Bd, H, Dh), _F32),
        compiler_params=_params(2),
        name="moba_sample",
    )(pt, q3, kn3, vn3, *([cache_k] * pps), *([cache_v] * pps))


def _oproj_kernel(ap_ref, as_ref, xp_ref, xs_ref, wo_ref, g_ref, b_ref, op_ref, os_ref,
                  *, npt, alpha):
    i = pl.program_id(0)

    def body(a_ref, x_ref, o_ref):
        m = jnp.dot(a_ref[...].astype(_BF16), wo_ref[...], preferred_element_type=_F32)
        o_ref[...] = _layer_norm(alpha * x_ref[...] + m, g_ref[...], b_ref[...])

    _dual(i, npt, lambda: body(ap_ref, xp_ref, op_ref), lambda: body(as_ref, xs_ref, os_ref))


def _oproj(ap, a_s, xp, xs, wo, g4, b4, layer, alpha, tm):
    S, D = xp.shape
    ns = xs.shape[0]
    AW = ap.shape[1]
    npt = S // tm
    vec = pl.BlockSpec((None, None, 1, D), lambda i: (layer, 1, 0, 0))
    return pl.pallas_call(
        functools.partial(_oproj_kernel, npt=npt, alpha=alpha),
        grid=(npt + 1,),
        in_specs=[pl.BlockSpec((tm, AW), _prompt_rows(npt)),
                  pl.BlockSpec((ns, AW), lambda i: (0, 0)),
                  pl.BlockSpec((tm, D), _prompt_rows(npt)),
                  pl.BlockSpec((ns, D), lambda i: (0, 0)),
                  pl.BlockSpec((AW, D), lambda i: (0, 0)), vec, vec],
        out_specs=[pl.BlockSpec((tm, D), _prompt_rows(npt)),
                   pl.BlockSpec((ns, D), lambda i: (0, 0))],
        out_shape=[jax.ShapeDtypeStruct((S, D), _F32), jax.ShapeDtypeStruct((ns, D), _F32)],
        compiler_params=_params(1),
        name="oproj_postnorm",
    )(ap, a_s, xp, xs, wo, g4, b4)


def kernel(x_prompt, x_sample, state_conv, cache_k, cache_v, page_table, ffn_w_gate, ffn_w_up, ffn_w_down, ln_g, ln_b, conv_w_pw1, conv_b_pw1, conv_w_dw, conv_b_dw, conv_ln_g, conv_ln_b, conv_w_pw2, conv_b_pw2, attn_w_q, attn_w_o, w_kv):
    B, S, D = x_prompt.shape
    Bd, Sd, _ = x_sample.shape
    depth = ffn_w_gate.shape[0]
    n_phys, page, H, Dh = cache_k.shape
    past_len = page_table.shape[1] * page
    conv_w = conv_w_dw.shape[1]
    assert B == 1 and Sd == 1 and depth == 2 and conv_w_pw1.shape[0] == 1
    assert S % MOBA_BLOCK == 0 and H * Dh == attn_w_q.shape[-1]
    alpha = (2.0 * depth) ** 0.25
    tm = _largest_tile(S, (512, 256))
    tm_ffn = _largest_tile(S, (1024, 512, 256))

    ffn_f32 = (ffn_w_gate, ffn_w_up, ffn_w_down)
    w16 = tuple(w[0, 0].astype(_BF16) for w in ffn_f32)
    g4, b4 = ln_g[:, :, None, :], ln_b[:, :, None, :]
    half = Dh // 2
    inv = ROPE_THETA ** (-jnp.arange(half, dtype=_F32) / half)
    inv128 = jnp.concatenate([inv, inv])[None, :]

    xp, xs = x_prompt.reshape(S, D), x_sample.reshape(Bd, D)

    xp, xs, w16, (w_pw1, w_pw2) = _ffn(xp, xs, w16, g4, b4, 0, 0, alpha, tm_ffn,
                                       ffn_f32 + (0, 1), (conv_w_pw1[0], conv_w_pw2[0]))
    up, us = _glu(xp, xs, w_pw1, conv_b_pw1, tm_ffn)
    xp, xs, conv_s = _conv_mixer(up, us, state_conv[0], xp, xs, conv_w_dw[0], conv_b_dw,
                                 conv_ln_g, conv_ln_b, w_pw2, conv_b_pw2,
                                 ln_g[0, 1:2], ln_b[0, 1:2], alpha)
    xp, xs, w16, (w_kv16,) = _ffn(xp, xs, w16, g4, b4, 0, 1, alpha, tm_ffn, ffn_f32 + (1, 0),
                                  (w_kv,))
    kp, vp, kbp, vt, kmean, ks, vs = _kv(xp, xs, w_kv16, inv128, past_len, tm_ffn)

    xp, xs, w16, (w_q, w_o) = _ffn(xp, xs, w16, g4, b4, 1, 0, alpha, tm_ffn, ffn_f32 + (1, 1),
                                   (attn_w_q[0], attn_w_o[0]))
    nb = S // MOBA_BLOCK
    qt, bias_t, qs = _q_proj(xp, xs, w_q, inv128, kmean, past_len, tm_ffn)
    hpb = _q_col_tile(H * Dh) // LANES
    ap = _attn_prompt(qt, kbp, vt, bias_t, Dh, hpb, nb)
    a_s = _attn_sample(qs.reshape(Bd, H, Dh), ks.reshape(Bd, H, Dh), vs.reshape(Bd, H, Dh),
                       cache_k, cache_v, page_table, past_len)
    xp, xs = _oproj(ap, a_s.reshape(Bd, H * Dh), xp, xs, w_o, g4, b4, 1, alpha, tm)
    xp, xs, _, _ = _ffn(xp, xs, w16, g4, b4, 1, 1, alpha, tm_ffn)

    conv_p = up[S - (conv_w - 1):].reshape(1, 1, conv_w - 1, D)
    return (xp.reshape(1, S, D), xs.reshape(Bd, 1, D), conv_p, conv_s[None],
            kp.reshape(1, S, H, Dh), vp.reshape(1, S, H, Dh),
            ks.reshape(Bd, 1, H, Dh), vs.reshape(Bd, 1, H, Dh))
```

```python
import functools
import math

import jax
import jax.numpy as jnp
from jax import lax
from jax.experimental import pallas as pl
from jax.experimental.pallas import tpu as pltpu

_F32 = jnp.float32
_BF16 = jnp.bfloat16

MOBA_BLOCK = 256
MOBA_TOPK = 3
ROPE_THETA = 10000.0
LN_EPS = 1e-5
LANES = 128
MASK_BIAS = -1e30
V7X_VMEM_LIMIT = 60 * 1024 * 1024

_NT = (((1,), (1,)), ((), ()))


def _params(n_grid_dims, vmem_bytes=V7X_VMEM_LIMIT):
    return pltpu.CompilerParams(
        dimension_semantics=("arbitrary",) * n_grid_dims, vmem_limit_bytes=vmem_bytes)


def _largest_tile(n, candidates):
    for c in candidates:
        if n % c == 0:
            return c
    return n


def _layer_norm(z, g, b):
    mu = jnp.mean(z, axis=-1, keepdims=True)
    zc = z - mu
    var = jnp.mean(zc * zc, axis=-1, keepdims=True)
    return zc * lax.rsqrt(var + LN_EPS) * g + b


def _sigmoid(a):
    return 1.0 / (1.0 + jnp.exp(-a))


def _dual(i, n_prompt_tiles, prompt_fn, sample_fn):
    pl.when(i < n_prompt_tiles)(prompt_fn)
    pl.when(i == n_prompt_tiles)(sample_fn)


def _prompt_rows(npt):
    return lambda i, *_: (jnp.minimum(i, npt - 1), 0)


def _prompt_tile(npt, nj):
    return lambda i, j: (jnp.minimum(i, npt - 1), jnp.where(i < npt, j, nj - 1))


def _sample_tile(npt):
    return lambda i, j: (0, jnp.where(i < npt, 0, j))


def _ffn_kernel(xp_ref, xs_ref, wg_ref, wu_ref, wd_ref, g_ref, b_ref, *refs, npt, ns, alpha,
                cast_cols, n_col_tiles):
    n_cast = len(cast_cols)
    cast_in, refs = refs[:n_cast], refs[n_cast:]
    op_ref, os_ref = refs[:2]
    cast_out, xb_ref = refs[2:2 + n_cast], refs[2 + n_cast]
    i, j, nj = pl.program_id(0), pl.program_id(1), pl.num_programs(1)

    def body(x_ref, o_ref, xb):
        @pl.when(j == 0)
        def _():
            xb[...] = x_ref[...].astype(_BF16)
            o_ref[...] = jnp.zeros_like(o_ref)

        x16 = xb[...]
        a = jnp.dot(x16, wg_ref[...], preferred_element_type=_F32)
        u = jnp.dot(x16, wu_ref[...], preferred_element_type=_F32)
        h = (a * _sigmoid(a)) * u
        o_ref[...] += jnp.dot(h.astype(_BF16), wd_ref[...], preferred_element_type=_F32)

        @pl.when(j == nj - 1)
        def _():
            z = alpha * x_ref[...] + 0.5 * o_ref[...]
            o_ref[...] = _layer_norm(z, g_ref[...], b_ref[...])

    def prompt_step():
        body(xp_ref, op_ref, xb_ref)
        for src, dst, n_cols in zip(cast_in, cast_out, cast_cols):
            def cast(src=src, dst=dst):
                dst[...] = src[...].astype(_BF16)
            if n_cols == n_col_tiles:
                cast()
            else:
                pl.when(j < n_cols)(cast)

    _dual(i, npt, prompt_step, lambda: body(xs_ref, os_ref, xb_ref.at[pl.ds(0, ns)]))


def _ffn(xp, xs, w16, g4, b4, layer, which, alpha, tm, next_f32=None, side_f32=()):
    norm = 2 * which
    wg, wu, wd = w16
    S, D = xp.shape
    ns = xs.shape[0]
    F = wg.shape[-1]
    tf = _largest_tile(F, (512, 256, 128))
    npt, nj = S // tm, F // tf
    w_in = pl.BlockSpec((D, tf), lambda i, j: (0, j))
    w_out = pl.BlockSpec((tf, D), lambda i, j: (j, 0))
    vec = pl.BlockSpec((None, None, 1, D), lambda i, j: (layer, norm, 0, 0))
    single = pl.Buffered(1)
    cast_args, cast_in, cast_out, cast_shapes, cast_cols = [], [], [], [], []
    tile = _prompt_tile(npt, nj)
    if next_f32 is not None:
        ng, nu, nd, nl, nw = next_f32
        rows = D // npt
        assert rows % LANES == 0
        cast_args = [ng, nu, nd]
        cast_cols = [nj, nj, nj]
        in_map = lambda i, j: (nl, nw) + tile(i, j)
        in_map_t = lambda i, j: (nl, nw) + tile(i, j)[::-1]
        cast_in = [pl.BlockSpec((None, None, rows, tf), in_map),
                   pl.BlockSpec((None, None, rows, tf), in_map),
                   pl.BlockSpec((None, None, tf, rows), in_map_t)]
        cast_out = [pl.BlockSpec((rows, tf), tile), pl.BlockSpec((rows, tf), tile),
                    pl.BlockSpec((tf, rows), lambda i, j: tile(i, j)[::-1])]
        cast_shapes = [jax.ShapeDtypeStruct((D, F), _BF16), jax.ShapeDtypeStruct((D, F), _BF16),
                       jax.ShapeDtypeStruct((F, D), _BF16)]
    n_next = len(cast_args)
    for w in side_f32:
        k_dim, n_dim = w.shape
        rows, n_cols = k_dim // npt, n_dim // tf
        assert k_dim % npt == 0 and rows % 16 == 0 and n_dim % tf == 0 and n_cols <= nj

        def side_map(i, j, n_cols=n_cols):
            r, c = tile(i, j)
            return r, jnp.minimum(c, n_cols - 1)

        cast_args.append(w)
        cast_cols.append(n_cols)
        cast_in.append(pl.BlockSpec((rows, tf), side_map))
        cast_out.append(pl.BlockSpec((rows, tf), side_map))
        cast_shapes.append(jax.ShapeDtypeStruct((k_dim, n_dim), _BF16))
    outs = pl.pallas_call(
        functools.partial(_ffn_kernel, npt=npt, ns=ns, alpha=alpha, cast_cols=tuple(cast_cols),
                          n_col_tiles=nj),
        grid=(npt + 1, nj),
        in_specs=[pl.BlockSpec((tm, D), _prompt_rows(npt), pipeline_mode=single),
                  pl.BlockSpec((ns, D), lambda i, j: (0, 0), pipeline_mode=single),
                  w_in, w_in, w_out, vec, vec] + cast_in,
        out_specs=[pl.BlockSpec((tm, D), _prompt_rows(npt), pipeline_mode=single),
                   pl.BlockSpec((ns, D), lambda i, j: (0, 0), pipeline_mode=single)] + cast_out,
        out_shape=[jax.ShapeDtypeStruct((S, D), _F32),
                   jax.ShapeDtypeStruct((ns, D), _F32)] + cast_shapes,
        scratch_shapes=[pltpu.VMEM((tm, D), _BF16)],
        compiler_params=_params(2),
        name="ffn_postnorm",
    )(xp, xs, wg, wu, wd, g4, b4, *cast_args)
    return outs[0], outs[1], tuple(outs[2:2 + n_next]), tuple(outs[2 + n_next:])


def _glu_kernel(xp_ref, xs_ref, wa_ref, wb_ref, ba_ref, bb_ref, op_ref, os_ref, xb_ref,
                *, npt, ns):
    i, j = pl.program_id(0), pl.program_id(1)

    def body(x_ref, o_ref, xb):
        @pl.when(j == 0)
        def _():
            xb[...] = x_ref[...].astype(_BF16)

        x16 = xb[...]
        a = jnp.dot(x16, wa_ref[...], preferred_element_type=_F32) + ba_ref[...]
        gate = jnp.dot(x16, wb_ref[...], preferred_element_type=_F32) + bb_ref[...]
        o_ref[...] = a * _sigmoid(gate)

    _dual(i, npt, lambda: body(xp_ref, op_ref, xb_ref),
          lambda: body(xs_ref, os_ref, xb_ref.at[pl.ds(0, ns)]))


def _glu(xp, xs, w_pw1, b_pw1, tm):
    S, D = xp.shape
    ns = xs.shape[0]
    tn = _largest_tile(D, (512, 256, 128))
    npt, nj = S // tm, D // tn
    b3 = b_pw1.reshape(1, 1, 2 * D)
    return pl.pallas_call(
        functools.partial(_glu_kernel, npt=npt, ns=ns),
        grid=(npt + 1, nj),
        in_specs=[pl.BlockSpec((tm, D), _prompt_rows(npt)),
                  pl.BlockSpec((ns, D), lambda i, j: (0, 0)),
                  pl.BlockSpec((D, tn), lambda i, j: (0, j)),
                  pl.BlockSpec((D, tn), lambda i, j: (0, j + nj)),
                  pl.BlockSpec((None, 1, tn), lambda i, j: (0, 0, j)),
                  pl.BlockSpec((None, 1, tn), lambda i, j: (0, 0, j + nj))],
        out_specs=[pl.BlockSpec((tm, tn), _prompt_tile(npt, nj)),
                   pl.BlockSpec((ns, tn), _sample_tile(npt))],
        out_shape=[jax.ShapeDtypeStruct((S, D), _F32), jax.ShapeDtypeStruct((ns, D), _F32)],
        scratch_shapes=[pltpu.VMEM((tm, D), _BF16)],
        compiler_params=_params(2),
        name="pw1_glu",
    )(xp, xs, w_pw1, w_pw1, b3, b3)


def _conv_tail(y, x1, gcn_ref, bcn_ref, w2_ref, b2_ref, g_ref, b_ref, alpha):
    z = _layer_norm(y, gcn_ref[...], bcn_ref[...])
    z = z * _sigmoid(z)
    m = jnp.dot(z.astype(_BF16), w2_ref[...], preferred_element_type=_F32) + b2_ref[...]
    return _layer_norm(alpha * x1 + m, g_ref[...], b_ref[...])


CONV_HALO = 32
CONV_ROWS = 64
CONV_COLS = 256
SUBLANES = 8


def _conv_prompt_kernel(u_ref, halo_ref, x1_ref, wdw_ref, bdw_ref, gcn_ref, bcn_ref, w2_ref,
                        b2_ref, g_ref, b_ref, o_ref, ext_ref, y_ref, wb_ref, *, conv_w, alpha):
    i = pl.program_id(0)
    tm, D = u_ref.shape

    @pl.when(i == 0)
    def _():
        for k in range(conv_w):
            wb_ref[k] = jnp.broadcast_to(wdw_ref[k:k + 1, :], (SUBLANES, D))

    ext_ref[0:CONV_HALO, :] = jnp.where(i == 0, 0.0, halo_ref[...])
    ext_ref[CONV_HALO:, :] = u_ref[...]
    lead = CONV_HALO - (conv_w - 1)
    cc = min(CONV_COLS, D)

    def row_chunk(r, carry):
        r0 = pl.multiple_of(r * CONV_ROWS, CONV_ROWS)
        for c in range(D // cc):
            cols = slice(c * cc, (c + 1) * cc)
            base = ext_ref[pl.ds(r0, CONV_ROWS + CONV_HALO), cols]
            acc = jnp.zeros((CONV_ROWS // SUBLANES, SUBLANES, cc), _F32)
            for shift in range(SUBLANES):
                offs = [o for o in range(shift, lead + conv_w, SUBLANES) if o >= lead]
                if not offs:
                    continue
                win = base[offs[0]:offs[-1] + CONV_ROWS]
                for o in offs:
                    tap = win[o - offs[0]:o - offs[0] + CONV_ROWS]
                    tap = tap.reshape(CONV_ROWS // SUBLANES, SUBLANES, cc)
                    acc = acc + wb_ref[o - lead, :, cols][None] * tap
            y_ref[pl.ds(r0, CONV_ROWS), cols] = acc.reshape(CONV_ROWS, cc) + bdw_ref[:, cols]
        return carry

    lax.fori_loop(0, tm // CONV_ROWS, row_chunk, 0)
    o_ref[...] = _conv_tail(y_ref[...], x1_ref[...], gcn_ref, bcn_ref, w2_ref, b2_ref,
                            g_ref, b_ref, alpha)


def _conv_sample_kernel(st_ref, u_ref, x1_ref, wdw_ref, bdw_ref, gcn_ref, bcn_ref, w2_ref,
                        b2_ref, g_ref, b_ref, o_ref, ost_ref, *, conv_w, alpha):
    w = wdw_ref[...]
    u = u_ref[...]
    y = jnp.sum(st_ref[...] * w[None, 0:conv_w - 1, :], axis=1)
    y = y + u * w[conv_w - 1:conv_w, :] + bdw_ref[...]
    o_ref[...] = _conv_tail(y, x1_ref[...], gcn_ref, bcn_ref, w2_ref, b2_ref, g_ref, b_ref,
                            alpha)
    ost_ref[:, 0:conv_w - 2, :] = st_ref[:, 1:conv_w - 1, :]
    ost_ref[:, conv_w - 2:conv_w - 1, :] = u[:, None, :]


def _conv_mixer(up, us, state, x1p, x1s, wdw, bdw, gcn, bcn, w2, b2, g, b, alpha):
    S, D = up.shape
    ns = us.shape[0]
    conv_w = wdw.shape[0]
    assert conv_w - 1 <= CONV_HALO
    tm = _largest_tile(S, (256, 128))
    per_halo = tm // CONV_HALO
    full = lambda shape: pl.BlockSpec(shape, lambda i: (0,) * len(shape))
    shared = [full((conv_w, D)), full((1, D)), full((1, D)), full((1, D)), full((D, D)),
              full((1, D)), full((1, D)), full((1, D))]
    shared_args = (wdw, bdw, gcn, bcn, w2, b2, g, b)
    xp = pl.pallas_call(
        functools.partial(_conv_prompt_kernel, conv_w=conv_w, alpha=alpha),
        grid=(S // tm,),
        in_specs=[pl.BlockSpec((tm, D), lambda i: (i, 0)),
                  pl.BlockSpec((CONV_HALO, D), lambda i: (jnp.maximum(i * per_halo - 1, 0), 0)),
                  pl.BlockSpec((tm, D), lambda i: (i, 0))] + shared,
        out_specs=pl.BlockSpec((tm, D), lambda i: (i, 0)),
        out_shape=jax.ShapeDtypeStruct((S, D), _F32),
        scratch_shapes=[pltpu.VMEM((CONV_HALO + tm, D), _F32), pltpu.VMEM((tm, D), _F32),
                        pltpu.VMEM((conv_w, SUBLANES, D), _F32)],
        compiler_params=_params(1),
        name="conv_prompt",
    )(up, up, x1p, *shared_args)
    tb = _largest_tile(ns, (16, 8))
    state_spec = pl.BlockSpec((None, tb, conv_w - 1, D), lambda i: (0, i, 0, 0))
    xs, new_state = pl.pallas_call(
        functools.partial(_conv_sample_kernel, conv_w=conv_w, alpha=alpha),
        grid=(ns // tb,),
        in_specs=[state_spec,
                  pl.BlockSpec((tb, D), lambda i: (i, 0)),
                  pl.BlockSpec((tb, D), lambda i: (i, 0))] + shared,
        out_specs=[pl.BlockSpec((tb, D), lambda i: (i, 0)), state_spec],
        out_shape=[jax.ShapeDtypeStruct((ns, D), _F32),
                   jax.ShapeDtypeStruct((1, ns, conv_w - 1, D), _F32)],
        compiler_params=_params(1),
        name="conv_sample",
    )(state, us, x1s, *shared_args)
    return xp, xs, new_state


def _rope_tables(pos, inv_ref, cos_ref, sin_ref):
    ang = pos * inv_ref[...]
    lane = lax.broadcasted_iota(jnp.int32, ang.shape, 1)
    cos_ref[...] = jnp.cos(ang)
    sin_ref[...] = jnp.where(lane < LANES // 2, -1.0, 1.0) * jnp.sin(ang)


def _rope_head(x, cos, sin):
    return x * cos + pltpu.roll(x, LANES // 2, axis=1) * sin


def _row_positions(i, tm):
    return (i * tm + lax.broadcasted_iota(jnp.int32, (tm, LANES), 0)).astype(_F32)


def _kv_kernel(xp_ref, xs_ref, wk_ref, wv_ref, inv_ref,
               kp_ref, vp_ref, kbp_ref, vbp_ref, km_ref, ks_ref, vs_ref,
               xb_ref, cos_ref, sin_ref, *, npt, ns, past_len):
    i, j = pl.program_id(0), pl.program_id(1)
    tn = wk_ref.shape[1]

    def body(x_ref, xb, cos_r, sin_r, pos_fn, k_ref, v_ref, kb_ref, vb_ref, kmean_ref):
        rows = x_ref.shape[0]

        @pl.when(j == 0)
        def _():
            xb[...] = x_ref[...].astype(_BF16)
            _rope_tables(pos_fn(rows), inv_ref, cos_r, sin_r)

        x16 = xb[...]
        k = jnp.dot(x16, wk_ref[...], preferred_element_type=_F32)
        v = jnp.dot(x16, wv_ref[...], preferred_element_type=_F32)
        cos, sin = cos_r[...], sin_r[...]
        k = jnp.concatenate(
            [_rope_head(k[:, h * LANES:(h + 1) * LANES], cos, sin) for h in range(tn // LANES)],
            axis=1)
        k_ref[...] = k
        v_ref[...] = v
        if kb_ref is not None:
            kb_ref[...] = k.astype(_BF16)
            nblk = rows // MOBA_BLOCK
            for b in range(nblk):
                vb_ref[b] = v[b * MOBA_BLOCK:(b + 1) * MOBA_BLOCK].T.astype(_BF16)
            means = [jnp.mean(k[b * MOBA_BLOCK:(b + 1) * MOBA_BLOCK], axis=0, keepdims=True)
                     for b in range(nblk)]
            means.append(jnp.zeros((kmean_ref.shape[0] - nblk, tn), _F32))
            kmean_ref[...] = jnp.concatenate(means, axis=0)

    _dual(i, npt,
          lambda: body(xp_ref, xb_ref, cos_ref, sin_ref, lambda rows: _row_positions(i, rows),
                       kp_ref, vp_ref, kbp_ref, vbp_ref, km_ref),
          lambda: body(xs_ref, xb_ref.at[pl.ds(0, ns)], cos_ref.at[pl.ds(0, ns)],
                       sin_ref.at[pl.ds(0, ns)],
                       lambda rows: jnp.full((rows, LANES), float(past_len), _F32),
                       ks_ref, vs_ref, None, None, None))


KMEAN_ROWS = 8


def _kv(xp, xs, w_kv, inv128, past_len, tm):
    S, D = xp.shape
    ns = xs.shape[0]
    AW = w_kv.shape[1] // 2
    tn = _largest_tile(AW, (512, 256, 128))
    npt, nj = S // tm, AW // tn
    assert tm % MOBA_BLOCK == 0 and tm // MOBA_BLOCK <= KMEAN_ROWS
    prow, srow = _prompt_tile(npt, nj), _sample_tile(npt)
    outs = pl.pallas_call(
        functools.partial(_kv_kernel, npt=npt, ns=ns, past_len=past_len),
        grid=(npt + 1, nj),
        in_specs=[pl.BlockSpec((tm, D), _prompt_rows(npt)),
                  pl.BlockSpec((ns, D), lambda i, j: (0, 0)),
                  pl.BlockSpec((D, tn), lambda i, j: (0, j)),
                  pl.BlockSpec((D, tn), lambda i, j: (0, j + nj)),
                  pl.BlockSpec((1, LANES), lambda i, j: (0, 0))],
        out_specs=[pl.BlockSpec((tm, tn), prow), pl.BlockSpec((tm, tn), prow),
                   pl.BlockSpec((tm, tn), prow),
                   pl.BlockSpec((tm // MOBA_BLOCK, tn, MOBA_BLOCK),
                                lambda i, j: (prow(i, j)[0], prow(i, j)[1], 0)),
                   pl.BlockSpec((None, KMEAN_ROWS, tn),
                                lambda i, j: (prow(i, j)[0], 0, prow(i, j)[1])),
                   pl.BlockSpec((ns, tn), srow), pl.BlockSpec((ns, tn), srow)],
        out_shape=[jax.ShapeDtypeStruct((S, AW), _F32), jax.ShapeDtypeStruct((S, AW), _F32),
                   jax.ShapeDtypeStruct((S, AW), _BF16),
                   jax.ShapeDtypeStruct((S // MOBA_BLOCK, AW, MOBA_BLOCK), _BF16),
                   jax.ShapeDtypeStruct((npt, KMEAN_ROWS, AW), _F32),
                   jax.ShapeDtypeStruct((ns, AW), _F32), jax.ShapeDtypeStruct((ns, AW), _F32)],
        scratch_shapes=[pltpu.VMEM((tm, D), _BF16), pltpu.VMEM((tm, LANES), _F32),
                        pltpu.VMEM((tm, LANES), _F32)],
        compiler_params=_params(2),
        name="kv_rope",
    )(xp, xs, w_kv, w_kv, inv128)
    kp, vp, kbp, vtp, km3, ks, vs = outs
    kmean = km3[:, :tm // MOBA_BLOCK].reshape(S // MOBA_BLOCK, AW)
    return kp, vp, kbp, vtp, kmean, ks, vs


def _select_bias(gate_t, query_block):
    nb = gate_t.shape[0]
    blk = lax.broadcasted_iota(jnp.int32, gate_t.shape, 0).astype(_F32)
    g = jnp.where(blk < query_block, gate_t, -jnp.inf)
    sel = jnp.zeros(gate_t.shape, jnp.bool_)
    for _ in range(MOBA_TOPK):
        mx = jnp.max(g, axis=0, keepdims=True)
        is_max = jnp.logical_and(g == mx, g > -jnp.inf)
        first = jnp.min(jnp.where(is_max, blk, float(nb)), axis=0, keepdims=True)
        pick = blk == first
        sel = jnp.logical_or(sel, pick)
        g = jnp.where(pick, -jnp.inf, g)
    return jnp.where(sel, 0.0, MASK_BIAS)


def _q_kernel(xp_ref, xs_ref, wq_ref, inv_ref, km_ref, qt_ref, bias_ref, qs_ref,
              xb_ref, cos_ref, sin_ref, *, npt, ns, past_len, score_scale):
    i, j = pl.program_id(0), pl.program_id(1)
    tn = wq_ref.shape[1]
    nb = km_ref.shape[0]

    def body(x_ref, xb, cos_r, sin_r, pos_fn, prompt):
        rows = x_ref.shape[0]

        @pl.when(j == 0)
        def _():
            xb[...] = x_ref[...].astype(_BF16)
            _rope_tables(pos_fn(rows), inv_ref, cos_r, sin_r)

        q = jnp.dot(xb[...], wq_ref[...], preferred_element_type=_F32)
        cos, sin = cos_r[...], sin_r[...]
        heads = [_rope_head(q[:, h * LANES:(h + 1) * LANES], cos, sin)
                 for h in range(tn // LANES)]
        q = jnp.concatenate(heads, axis=1)
        if not prompt:
            qs_ref[...] = q
            return
        qt_ref[...] = (q * score_scale).T.astype(_BF16)
        query_block = (i * rows + lax.broadcasted_iota(jnp.int32, (1, rows), 1)) // MOBA_BLOCK
        query_block = query_block.astype(_F32)
        for h, qh in enumerate(heads):
            gate_t = lax.dot_general(km_ref[:, h * LANES:(h + 1) * LANES], qh, _NT,
                                     preferred_element_type=_F32,
                                     precision=lax.Precision.HIGHEST)
            bias_ref[h * nb:(h + 1) * nb, :] = _select_bias(gate_t, query_block)

    _dual(i, npt,
          lambda: body(xp_ref, xb_ref, cos_ref, sin_ref, lambda rows: _row_positions(i, rows),
                       True),
          lambda: body(xs_ref, xb_ref.at[pl.ds(0, ns)], cos_ref.at[pl.ds(0, ns)],
                       sin_ref.at[pl.ds(0, ns)],
                       lambda rows: jnp.full((rows, LANES), float(past_len), _F32), False))


def _q_col_tile(attn_width):
    return _largest_tile(attn_width, (512, 256, 128))


def _q_proj(xp, xs, wq, inv128, kmean, past_len, tm):
    S, D = xp.shape
    ns = xs.shape[0]
    AW = wq.shape[-1]
    nb = kmean.shape[0]
    tn = _q_col_tile(AW)
    hpb = tn // LANES
    npt, nj = S // tm, AW // tn
    assert nj == 1 or (hpb * nb) % SUBLANES == 0
    prow = _prompt_tile(npt, nj)
    pcol = lambda i, j: prow(i, j)[::-1]
    return pl.pallas_call(
        functools.partial(_q_kernel, npt=npt, ns=ns, past_len=past_len,
                          score_scale=LANES ** -0.5 * math.log2(math.e)),
        grid=(npt + 1, nj),
        in_specs=[pl.BlockSpec((tm, D), _prompt_rows(npt)),
                  pl.BlockSpec((ns, D), lambda i, j: (0, 0)),
                  pl.BlockSpec((D, tn), lambda i, j: (0, j)),
                  pl.BlockSpec((1, LANES), lambda i, j: (0, 0)),
                  pl.BlockSpec((nb, tn), lambda i, j: (0, j))],
        out_specs=[pl.BlockSpec((tn, tm), pcol),
                   pl.BlockSpec((hpb * nb, tm), pcol),
                   pl.BlockSpec((ns, tn), _sample_tile(npt))],
        out_shape=[jax.ShapeDtypeStruct((AW, S), _BF16),
                   jax.ShapeDtypeStruct(((AW // LANES) * nb, S), _F32),
                   jax.ShapeDtypeStruct((ns, AW), _F32)],
        scratch_shapes=[pltpu.VMEM((tm, D), _BF16), pltpu.VMEM((tm, LANES), _F32),
                        pltpu.VMEM((tm, LANES), _F32)],
        compiler_params=_params(2),
        name="q_rope_select",
    )(xp, xs, wq, inv128, kmean)


def _attn_prompt_kernel(qt_ref, k_ref, vt_ref, bias_ref, o_ref, qa_ref, sa_ref, sb_ref, m_ref,
                        l_ref, acc_ref, *, nb):
    qi = pl.program_id(1)
    tq = qt_ref.shape[1]
    n_heads = qt_ref.shape[0] // LANES

    heads_per_slab = min(n_heads, LANES // nb)
    slabs = []
    for first in range(0, n_heads, heads_per_slab):
        slab = bias_ref[first * nb:(first + heads_per_slab) * nb, :].astype(_BF16)
        if slab.shape[0] < LANES:
            slab = jnp.concatenate(
                [slab, jnp.zeros((LANES - slab.shape[0], tq), _BF16)], axis=0)
        slabs.append(slab)
    key_i = lax.broadcasted_iota(jnp.int32, (MOBA_BLOCK, tq), 0)
    qry_i = lax.broadcasted_iota(jnp.int32, (MOBA_BLOCK, tq), 1)
    j0 = pl.multiple_of(qi * MOBA_BLOCK, MOBA_BLOCK)
    head_cols = [slice(g * LANES, (g + 1) * LANES) for g in range(n_heads)]
    scores = [jnp.dot(k_ref[pl.ds(j0, MOBA_BLOCK), cols], qt_ref[cols, :],
                      preferred_element_type=_F32) for cols in head_cols]
    probs = []
    for g in range(n_heads):
        s = jnp.where(key_i <= qry_i, scores[g], -jnp.inf)
        m = jnp.max(s, axis=0, keepdims=True)
        p = jnp.exp2(s - m)
        m_ref[g] = m
        l_ref[g] = jnp.sum(p, axis=0, keepdims=True)
        probs.append(p.astype(_BF16))
    for g, cols in enumerate(head_cols):
        acc_ref[g] = jnp.dot(vt_ref[qi, cols, :], probs[g], preferred_element_type=_F32)
        qa_ref[g] = jnp.concatenate([qt_ref[cols, :], slabs[g // heads_per_slab]], axis=0)

    lane = lax.broadcasted_iota(jnp.int32, (MOBA_BLOCK, LANES), 1)

    def past_scores(j):
        k0 = pl.multiple_of(j * MOBA_BLOCK, MOBA_BLOCK)
        out = []
        for g, cols in enumerate(head_cols):
            onehot = jnp.where(lane == (g % heads_per_slab) * nb + j, 1.0, 0.0).astype(_BF16)
            k_aug = jnp.concatenate([k_ref[pl.ds(k0, MOBA_BLOCK), cols], onehot], axis=1)
            out.append(jnp.dot(k_aug, qa_ref[g], preferred_element_type=_F32))
        return out

    def compute_scores(j, s_ref):
        for g, s in enumerate(past_scores(j)):
            s_ref[g] = s

    def consume_scores(j, s_ref):
        probs, corrs = [], []
        for g in range(n_heads):
            s = s_ref[g]
            m_old = m_ref[g]
            m_new = jnp.maximum(m_old, jnp.max(s, axis=0, keepdims=True))
            corr = jnp.exp2(m_old - m_new)
            pj = jnp.exp2(s - m_new)
            l_ref[g] = corr * l_ref[g] + jnp.sum(pj, axis=0, keepdims=True)
            m_ref[g] = m_new
            probs.append(pj.astype(_BF16))
            corrs.append(corr)
        for g, cols in enumerate(head_cols):
            acc_ref[g] = corrs[g] * acc_ref[g] + jnp.dot(vt_ref[j, cols, :], probs[g],
                                                         preferred_element_type=_F32)

    last = nb - 1
    compute_scores(0, sa_ref)

    def step(t, carry):
        j = 2 * t
        compute_scores(jnp.minimum(j + 1, last), sb_ref)
        consume_scores(j, sa_ref)
        compute_scores(jnp.minimum(j + 2, last), sa_ref)
        consume_scores(jnp.minimum(j + 1, last), sb_ref)
        return carry

    lax.fori_loop(0, (qi + 1) // 2, step, 0)
    for g in range(n_heads):
        out_t = acc_ref[g] / l_ref[g]
        o_ref[:, g * LANES:(g + 1) * LANES] = out_t.T.astype(o_ref.dtype)


ATTN_HEADS_PER_STEP = 4


def _attn_prompt(qt, kb, vt, bias_t, head_dim, nb):
    AW, S = qt.shape
    assert head_dim == LANES and nb <= LANES
    tq = MOBA_BLOCK
    hpb = _largest_tile(AW // head_dim, (ATTN_HEADS_PER_STEP, 4, 2, 1))
    assert hpb % min(hpb, LANES // nb) == 0
    gw = hpb * head_dim
    return pl.pallas_call(
        functools.partial(_attn_prompt_kernel, nb=nb),
        grid=(AW // gw, S // tq),
        in_specs=[pl.BlockSpec((gw, tq), lambda h, i: (h, i)),
                  pl.BlockSpec((S, gw), lambda h, i: (0, h)),
                  pl.BlockSpec((nb, gw, MOBA_BLOCK), lambda h, i: (0, h, 0)),
                  pl.BlockSpec((hpb * nb, tq), lambda h, i: (h, i))],
        out_specs=pl.BlockSpec((tq, gw), lambda h, i: (i, h)),
        out_shape=jax.ShapeDtypeStruct((S, AW), _BF16),
        scratch_shapes=[pltpu.VMEM((hpb, 2 * LANES, tq), _BF16),
                        pltpu.VMEM((hpb, MOBA_BLOCK, tq), _F32),
                        pltpu.VMEM((hpb, MOBA_BLOCK, tq), _F32),
                        pltpu.VMEM((hpb, 1, tq), _F32), pltpu.VMEM((hpb, 1, tq), _F32),
                        pltpu.VMEM((hpb, head_dim, tq), _F32)],
        compiler_params=_params(2),
        name="moba_prompt",
    )(qt, kb, vt, bias_t)


PAGES_PER_BLOCK = 2


def _attn_sample_kernel(pt_ref, q_ref, kn_ref, vn_ref, *refs, n_blocks, blocks_per_step,
                        scale_log2e):
    del pt_ref
    n_pages = PAGES_PER_BLOCK * blocks_per_step
    k_refs, v_refs = refs[:n_pages], refs[n_pages:2 * n_pages]
    o_ref, m_ref, l_ref, g_ref, acc_ref = refs[2 * n_pages:]
    step = pl.program_id(1)
    page, H, Dh = k_refs[0].shape
    rows = page * H
    q = q_ref[...]
    q_hi = q.astype(_BF16)
    q_lo = (q - q_hi.astype(_F32)).astype(_BF16)
    q2 = jnp.concatenate([q_hi, q_lo], axis=0)

    def scores(k_ref):
        k16 = k_ref[...].reshape(rows, Dh).astype(_BF16)
        s2 = lax.dot_general(q2, k16, _NT, preferred_element_type=_F32)
        return s2[0:H] + s2[H:2 * H]

    head_of_row = lax.broadcasted_iota(jnp.int32, (H, PAGES_PER_BLOCK * rows), 1) % H
    own = head_of_row == lax.broadcasted_iota(jnp.int32, (H, PAGES_PER_BLOCK * rows), 0)
    for t in range(blocks_per_step):
        k0_ref, k1_ref = k_refs[2 * t], k_refs[2 * t + 1]
        v0_ref, v1_ref = v_refs[2 * t], v_refs[2 * t + 1]
        s = jnp.concatenate([scores(k0_ref), scores(k1_ref)], axis=1)
        sm = jnp.where(own, s, -jnp.inf)
        m = jnp.max(sm, axis=1, keepdims=True)
        p = jnp.exp2((sm - m) * scale_log2e)
        l = jnp.sum(p, axis=1, keepdims=True)
        g = jnp.sum(jnp.where(own, s, 0.0), axis=1, keepdims=True)
        p16 = p.astype(_BF16)
        o = (jnp.dot(p16[:, 0:rows], v0_ref[...].reshape(rows, Dh).astype(_BF16),
                     preferred_element_type=_F32)
             + jnp.dot(p16[:, rows:2 * rows], v1_ref[...].reshape(rows, Dh).astype(_BF16),
                       preferred_element_type=_F32))
        n = step * blocks_per_step + t
        m_ref[n] = jnp.broadcast_to(m, (H, LANES))
        l_ref[n] = jnp.broadcast_to(l, (H, LANES))
        g_ref[n] = jnp.broadcast_to(g, (H, LANES))
        acc_ref[n] = o

    @pl.when(step == n_blocks // blocks_per_step - 1)
    def _():
        gates = [g_ref[b] for b in range(n_blocks)]
        s_own = jnp.sum(q * kn_ref[...], axis=1, keepdims=True)
        m_tot = jnp.broadcast_to(s_own, (H, LANES))
        sel = []
        for b in range(n_blocks):
            rank = jnp.zeros((H, LANES), _F32)
            for c in range(n_blocks):
                if c == b:
                    continue
                ahead = gates[c] >= gates[b] if c < b else gates[c] > gates[b]
                rank = rank + jnp.where(ahead, 1.0, 0.0)
            sel.append(rank < float(MOBA_TOPK))
            m_tot = jnp.where(sel[b], jnp.maximum(m_tot, m_ref[b]), m_tot)
        w_own = jnp.exp2((s_own - m_tot) * scale_log2e)
        den = w_own
        num = w_own * vn_ref[...]
        for b in range(n_blocks):
            w = jnp.where(sel[b], jnp.exp2((m_ref[b] - m_tot) * scale_log2e), 0.0)
            den = den + w * l_ref[b]
            num = num + w * acc_ref[b]
        o_ref[...] = num / den


def _attn_sample(q3, kn3, vn3, cache_k, cache_v, page_table, past_len):
    Bd, H, Dh = q3.shape
    page = cache_k.shape[1]
    assert Dh == LANES and MOBA_BLOCK == PAGES_PER_BLOCK * page
    assert past_len % MOBA_BLOCK == 0
    n_blocks = past_len // MOBA_BLOCK
    assert n_blocks >= MOBA_TOPK
    bps = _largest_tile(n_blocks, (4, 2, 1))
    pps = PAGES_PER_BLOCK * bps
    n_pages = page_table.shape[1]
    pt = page_table.reshape(-1)
    tok = pl.BlockSpec((None, H, Dh), lambda b, n, pt: (b, 0, 0))

    def page_spec(which):
        return pl.BlockSpec((None, page, H, Dh),
                            lambda b, n, pt: (pt[b * n_pages + pps * n + which], 0, 0, 0))

    pages = [page_spec(w) for w in range(pps)]
    grid_spec = pltpu.PrefetchScalarGridSpec(
        num_scalar_prefetch=1,
        grid=(Bd, n_blocks // bps),
        in_specs=[tok, tok, tok] + pages + pages,
        out_specs=tok,
        scratch_shapes=[pltpu.VMEM((n_blocks, H, LANES), _F32),
                        pltpu.VMEM((n_blocks, H, LANES), _F32),
                        pltpu.VMEM((n_blocks, H, LANES), _F32),
                        pltpu.VMEM((n_blocks, H, Dh), _F32)],
    )
    return pl.pallas_call(
        functools.partial(_attn_sample_kernel, n_blocks=n_blocks, blocks_per_step=bps,
                          scale_log2e=Dh ** -0.5 * math.log2(math.e)),
        grid_spec=grid_spec,
        out_shape=jax.ShapeDtypeStruct((Bd, H, Dh), _F32),
        compiler_params=_params(2),
        name="moba_sample",
    )(pt, q3, kn3, vn3, *([cache_k] * pps), *([cache_v] * pps))


def _oproj_kernel(ap_ref, as_ref, xp_ref, xs_ref, wo_ref, g_ref, b_ref, op_ref, os_ref,
                  *, npt, alpha):
    i = pl.program_id(0)

    def body(a_ref, x_ref, o_ref):
        m = jnp.dot(a_ref[...].astype(_BF16), wo_ref[...], preferred_element_type=_F32)
        o_ref[...] = _layer_norm(alpha * x_ref[...] + m, g_ref[...], b_ref[...])

    _dual(i, npt, lambda: body(ap_ref, xp_ref, op_ref), lambda: body(as_ref, xs_ref, os_ref))


def _oproj(ap, a_s, xp, xs, wo, g4, b4, layer, alpha, tm):
    S, D = xp.shape
    ns = xs.shape[0]
    AW = ap.shape[1]
    npt = S // tm
    vec = pl.BlockSpec((None, None, 1, D), lambda i: (layer, 1, 0, 0))
    return pl.pallas_call(
        functools.partial(_oproj_kernel, npt=npt, alpha=alpha),
        grid=(npt + 1,),
        in_specs=[pl.BlockSpec((tm, AW), _prompt_rows(npt)),
                  pl.BlockSpec((ns, AW), lambda i: (0, 0)),
                  pl.BlockSpec((tm, D), _prompt_rows(npt)),
                  pl.BlockSpec((ns, D), lambda i: (0, 0)),
                  pl.BlockSpec((AW, D), lambda i: (0, 0)), vec, vec],
        out_specs=[pl.BlockSpec((tm, D), _prompt_rows(npt)),
                   pl.BlockSpec((ns, D), lambda i: (0, 0))],
        out_shape=[jax.ShapeDtypeStruct((S, D), _F32), jax.ShapeDtypeStruct((ns, D), _F32)],
        compiler_params=_params(1),
        name="oproj_postnorm",
    )(ap, a_s, xp, xs, wo, g4, b4)


def kernel(x_prompt, x_sample, state_conv, cache_k, cache_v, page_table, ffn_w_gate, ffn_w_up, ffn_w_down, ln_g, ln_b, conv_w_pw1, conv_b_pw1, conv_w_dw, conv_b_dw, conv_ln_g, conv_ln_b, conv_w_pw2, conv_b_pw2, attn_w_q, attn_w_o, w_kv):
    B, S, D = x_prompt.shape
    Bd, Sd, _ = x_sample.shape
    depth = ffn_w_gate.shape[0]
    n_phys, page, H, Dh = cache_k.shape
    past_len = page_table.shape[1] * page
    conv_w = conv_w_dw.shape[1]
    assert B == 1 and Sd == 1 and depth == 2 and conv_w_pw1.shape[0] == 1
    assert S % MOBA_BLOCK == 0 and H * Dh == attn_w_q.shape[-1]
    alpha = (2.0 * depth) ** 0.25
    tm = _largest_tile(S, (512, 256))
    tm_ffn = _largest_tile(S, (1024, 512, 256))

    ffn_f32 = (ffn_w_gate, ffn_w_up, ffn_w_down)
    w16 = tuple(w[0, 0].astype(_BF16) for w in ffn_f32)
    g4, b4 = ln_g[:, :, None, :], ln_b[:, :, None, :]
    half = Dh // 2
    inv = ROPE_THETA ** (-jnp.arange(half, dtype=_F32) / half)
    inv128 = jnp.concatenate([inv, inv])[None, :]

    xp, xs = x_prompt.reshape(S, D), x_sample.reshape(Bd, D)

    xp, xs, w16, (w_pw1, w_pw2) = _ffn(xp, xs, w16, g4, b4, 0, 0, alpha, tm_ffn,
                                       ffn_f32 + (0, 1), (conv_w_pw1[0], conv_w_pw2[0]))
    up, us = _glu(xp, xs, w_pw1, conv_b_pw1, tm_ffn)
    xp, xs, conv_s = _conv_mixer(up, us, state_conv, xp, xs, conv_w_dw[0], conv_b_dw,
                                 conv_ln_g, conv_ln_b, w_pw2, conv_b_pw2,
                                 ln_g[0, 1:2], ln_b[0, 1:2], alpha)
    xp, xs, w16, (w_kv16,) = _ffn(xp, xs, w16, g4, b4, 0, 1, alpha, tm_ffn, ffn_f32 + (1, 0),
                                  (w_kv,))
    kp, vp, kbp, vt, kmean, ks, vs = _kv(xp, xs, w_kv16, inv128, past_len, tm_ffn)

    xp, xs, w16, (w_q, w_o) = _ffn(xp, xs, w16, g4, b4, 1, 0, alpha, tm_ffn, ffn_f32 + (1, 1),
                                   (attn_w_q[0], attn_w_o[0]))
    nb = S // MOBA_BLOCK
    qt, bias_t, qs = _q_proj(xp, xs, w_q, inv128, kmean, past_len, tm_ffn)
    ap = _attn_prompt(qt, kbp, vt, bias_t, Dh, nb)
    a_s = _attn_sample(qs.reshape(Bd, H, Dh), ks.reshape(Bd, H, Dh), vs.reshape(Bd, H, Dh),
                       cache_k, cache_v, page_table, past_len)
    xp, xs = _oproj(ap, a_s.reshape(Bd, H * Dh), xp, xs, w_o, g4, b4, 1, alpha, tm)
    xp, xs, _, _ = _ffn(xp, xs, w16, g4, b4, 1, 1, alpha, tm_ffn)

    conv_p = up[S - (conv_w - 1):].reshape(1, 1, conv_w - 1, D)
    return (xp.reshape(1, S, D), xs.reshape(Bd, 1, D), conv_p, conv_s,
            kp.reshape(1, S, H, Dh), vp.reshape(1, S, H, Dh),
            ks.reshape(Bd, 1, H, Dh), vs.reshape(Bd, 1, H, Dh))
```

```python
import functools
import math

import jax
import jax.numpy as jnp
from jax import lax
from jax.experimental import pallas as pl
from jax.experimental.pallas import tpu as pltpu

_F32 = jnp.float32
_BF16 = jnp.bfloat16

MOBA_BLOCK = 256
MOBA_TOPK = 3
ROPE_THETA = 10000.0
LN_EPS = 1e-5
LANES = 128
MASK_BIAS = -1e30
V7X_VMEM_LIMIT = 60 * 1024 * 1024

_NT = (((1,), (1,)), ((), ()))


def _params(n_grid_dims, vmem_bytes=V7X_VMEM_LIMIT):
    return pltpu.CompilerParams(
        dimension_semantics=("arbitrary",) * n_grid_dims, vmem_limit_bytes=vmem_bytes)


def _largest_tile(n, candidates):
    for c in candidates:
        if n % c == 0:
            return c
    return n


def _layer_norm(z, g, b):
    mu = jnp.mean(z, axis=-1, keepdims=True)
    zc = z - mu
    var = jnp.mean(zc * zc, axis=-1, keepdims=True)
    return zc * lax.rsqrt(var + LN_EPS) * g + b


def _sigmoid(a):
    return 1.0 / (1.0 + jnp.exp(-a))


def _dual(i, n_prompt_tiles, prompt_fn, sample_fn):
    pl.when(i < n_prompt_tiles)(prompt_fn)
    pl.when(i == n_prompt_tiles)(sample_fn)


def _prompt_rows(npt):
    return lambda i, *_: (jnp.minimum(i, npt - 1), 0)


def _prompt_tile(npt, nj):
    return lambda i, j: (jnp.minimum(i, npt - 1), jnp.where(i < npt, j, nj - 1))


def _sample_tile(npt):
    return lambda i, j: (0, jnp.where(i < npt, 0, j))


def _ffn_kernel(xp_ref, xs_ref, wg_ref, wu_ref, wd_ref, g_ref, b_ref, *refs, npt, ns, alpha,
                n_cast):
    cast_in, refs = refs[:n_cast], refs[n_cast:]
    op_ref, os_ref = refs[:2]
    cast_out, xb_ref = refs[2:2 + n_cast], refs[2 + n_cast]
    i, j, nj = pl.program_id(0), pl.program_id(1), pl.num_programs(1)

    def body(x_ref, o_ref, xb):
        @pl.when(j == 0)
        def _():
            xb[...] = x_ref[...].astype(_BF16)
            o_ref[...] = jnp.zeros_like(o_ref)

        x16 = xb[...]
        a = jnp.dot(x16, wg_ref[...], preferred_element_type=_F32)
        u = jnp.dot(x16, wu_ref[...], preferred_element_type=_F32)
        h = (a * _sigmoid(a)) * u
        o_ref[...] += jnp.dot(h.astype(_BF16), wd_ref[...], preferred_element_type=_F32)

        @pl.when(j == nj - 1)
        def _():
            z = alpha * x_ref[...] + 0.5 * o_ref[...]
            o_ref[...] = _layer_norm(z, g_ref[...], b_ref[...])

    def prompt_step():
        body(xp_ref, op_ref, xb_ref)
        for src, dst in zip(cast_in, cast_out):
            dst[...] = src[...].astype(_BF16)

    _dual(i, npt, prompt_step, lambda: body(xs_ref, os_ref, xb_ref.at[pl.ds(0, ns)]))


def _ffn(xp, xs, w16, g4, b4, layer, which, alpha, tm, next_f32=None, side_f32=()):
    norm = 2 * which
    wg, wu, wd = w16
    S, D = xp.shape
    ns = xs.shape[0]
    F = wg.shape[-1]
    tf = _largest_tile(F, (512, 256, 128))
    npt, nj = S // tm, F // tf
    w_in = pl.BlockSpec((D, tf), lambda i, j: (0, j))
    w_out = pl.BlockSpec((tf, D), lambda i, j: (j, 0))
    vec = pl.BlockSpec((None, None, 1, D), lambda i, j: (layer, norm, 0, 0))
    single = pl.Buffered(1)
    cast_args, cast_in, cast_out, cast_shapes = [], [], [], []
    tile = _prompt_tile(npt, nj)
    if next_f32 is not None:
        ng, nu, nd, nl, nw = next_f32
        rows = D // npt
        assert rows % LANES == 0
        cast_args = [ng, nu, nd]
        in_map = lambda i, j: (nl, nw) + tile(i, j)
        in_map_t = lambda i, j: (nl, nw) + tile(i, j)[::-1]
        cast_in = [pl.BlockSpec((None, None, rows, tf), in_map),
                   pl.BlockSpec((None, None, rows, tf), in_map),
                   pl.BlockSpec((None, None, tf, rows), in_map_t)]
        cast_out = [pl.BlockSpec((rows, tf), tile), pl.BlockSpec((rows, tf), tile),
                    pl.BlockSpec((tf, rows), lambda i, j: tile(i, j)[::-1])]
        cast_shapes = [jax.ShapeDtypeStruct((D, F), _BF16), jax.ShapeDtypeStruct((D, F), _BF16),
                       jax.ShapeDtypeStruct((F, D), _BF16)]
    n_next = len(cast_args)
    for w in side_f32:
        k_dim, n_dim = w.shape
        rows, n_cols = k_dim // npt, n_dim // tf
        assert k_dim % npt == 0 and rows % 16 == 0 and n_dim % tf == 0 and n_cols <= nj

        def side_map(i, j, n_cols=n_cols):
            r, c = tile(i, j)
            return r, jnp.minimum(c, n_cols - 1)

        cast_args.append(w)
        cast_in.append(pl.BlockSpec((rows, tf), side_map))
        cast_out.append(pl.BlockSpec((rows, tf), side_map))
        cast_shapes.append(jax.ShapeDtypeStruct((k_dim, n_dim), _BF16))
    outs = pl.pallas_call(
        functools.partial(_ffn_kernel, npt=npt, ns=ns, alpha=alpha, n_cast=len(cast_args)),
        grid=(npt + 1, nj),
        in_specs=[pl.BlockSpec((tm, D), _prompt_rows(npt), pipeline_mode=single),
                  pl.BlockSpec((ns, D), lambda i, j: (0, 0), pipeline_mode=single),
                  w_in, w_in, w_out, vec, vec] + cast_in,
        out_specs=[pl.BlockSpec((tm, D), _prompt_rows(npt), pipeline_mode=single),
                   pl.BlockSpec((ns, D), lambda i, j: (0, 0), pipeline_mode=single)] + cast_out,
        out_shape=[jax.ShapeDtypeStruct((S, D), _F32),
                   jax.ShapeDtypeStruct((ns, D), _F32)] + cast_shapes,
        scratch_shapes=[pltpu.VMEM((tm, D), _BF16)],
        compiler_params=_params(2),
        name="ffn_postnorm",
    )(xp, xs, wg, wu, wd, g4, b4, *cast_args)
    return outs[0], outs[1], tuple(outs[2:2 + n_next]), tuple(outs[2 + n_next:])


def _glu_kernel(xp_ref, xs_ref, wa_ref, wb_ref, ba_ref, bb_ref, op_ref, os_ref, xb_ref,
                *, npt, ns):
    i, j = pl.program_id(0), pl.program_id(1)

    def body(x_ref, o_ref, xb):
        @pl.when(j == 0)
        def _():
            xb[...] = x_ref[...].astype(_BF16)

        x16 = xb[...]
        a = jnp.dot(x16, wa_ref[...], preferred_element_type=_F32) + ba_ref[...]
        gate = jnp.dot(x16, wb_ref[...], preferred_element_type=_F32) + bb_ref[...]
        o_ref[...] = a * _sigmoid(gate)

    _dual(i, npt, lambda: body(xp_ref, op_ref, xb_ref),
          lambda: body(xs_ref, os_ref, xb_ref.at[pl.ds(0, ns)]))


def _glu(xp, xs, w_pw1, b_pw1, tm):
    S, D = xp.shape
    ns = xs.shape[0]
    tn = _largest_tile(D, (512, 256, 128))
    npt, nj = S // tm, D // tn
    b3 = b_pw1.reshape(1, 1, 2 * D)
    return pl.pallas_call(
        functools.partial(_glu_kernel, npt=npt, ns=ns),
        grid=(npt + 1, nj),
        in_specs=[pl.BlockSpec((tm, D), _prompt_rows(npt)),
                  pl.BlockSpec((ns, D), lambda i, j: (0, 0)),
                  pl.BlockSpec((D, tn), lambda i, j: (0, j)),
                  pl.BlockSpec((D, tn), lambda i, j: (0, j + nj)),
                  pl.BlockSpec((None, 1, tn), lambda i, j: (0, 0, j)),
                  pl.BlockSpec((None, 1, tn), lambda i, j: (0, 0, j + nj))],
        out_specs=[pl.BlockSpec((tm, tn), _prompt_tile(npt, nj)),
                   pl.BlockSpec((ns, tn), _sample_tile(npt))],
        out_shape=[jax.ShapeDtypeStruct((S, D), _F32), jax.ShapeDtypeStruct((ns, D), _F32)],
        scratch_shapes=[pltpu.VMEM((tm, D), _BF16)],
        compiler_params=_params(2),
        name="pw1_glu",
    )(xp, xs, w_pw1, w_pw1, b3, b3)


def _conv_tail(y, x1, gcn_ref, bcn_ref, w2_ref, b2_ref, g_ref, b_ref, alpha):
    z = _layer_norm(y, gcn_ref[...], bcn_ref[...])
    z = z * _sigmoid(z)
    m = jnp.dot(z.astype(_BF16), w2_ref[...], preferred_element_type=_F32) + b2_ref[...]
    return _layer_norm(alpha * x1 + m, g_ref[...], b_ref[...])


CONV_HALO = 32
CONV_ROWS = 64
CONV_COLS = 256
SUBLANES = 8


def _conv_prompt_kernel(u_ref, halo_ref, x1_ref, wdw_ref, bdw_ref, gcn_ref, bcn_ref, w2_ref,
                        b2_ref, g_ref, b_ref, o_ref, ext_ref, y_ref, wb_ref, *, conv_w, alpha):
    i = pl.program_id(0)
    tm, D = u_ref.shape

    @pl.when(i == 0)
    def _():
        for k in range(conv_w):
            wb_ref[k] = jnp.broadcast_to(wdw_ref[k:k + 1, :], (SUBLANES, D))

    ext_ref[0:CONV_HALO, :] = jnp.where(i == 0, 0.0, halo_ref[...])
    ext_ref[CONV_HALO:, :] = u_ref[...]
    lead = CONV_HALO - (conv_w - 1)
    cc = min(CONV_COLS, D)

    def row_chunk(r, carry):
        r0 = pl.multiple_of(r * CONV_ROWS, CONV_ROWS)
        for c in range(D // cc):
            cols = slice(c * cc, (c + 1) * cc)
            base = ext_ref[pl.ds(r0, CONV_ROWS + CONV_HALO), cols]
            acc = jnp.zeros((CONV_ROWS // SUBLANES, SUBLANES, cc), _F32)
            for shift in range(SUBLANES):
                offs = [o for o in range(shift, lead + conv_w, SUBLANES) if o >= lead]
                if not offs:
                    continue
                win = base[offs[0]:offs[-1] + CONV_ROWS]
                for o in offs:
                    tap = win[o - offs[0]:o - offs[0] + CONV_ROWS]
                    tap = tap.reshape(CONV_ROWS // SUBLANES, SUBLANES, cc)
                    acc = acc + wb_ref[o - lead, :, cols][None] * tap
            y_ref[pl.ds(r0, CONV_ROWS), cols] = acc.reshape(CONV_ROWS, cc) + bdw_ref[:, cols]
        return carry

    lax.fori_loop(0, tm // CONV_ROWS, row_chunk, 0)
    o_ref[...] = _conv_tail(y_ref[...], x1_ref[...], gcn_ref, bcn_ref, w2_ref, b2_ref,
                            g_ref, b_ref, alpha)


def _conv_sample_kernel(st_ref, u_ref, x1_ref, wdw_ref, bdw_ref, gcn_ref, bcn_ref, w2_ref,
                        b2_ref, g_ref, b_ref, o_ref, ost_ref, *, conv_w, alpha):
    w = wdw_ref[...]
    u = u_ref[...]
    y = jnp.sum(st_ref[...] * w[None, 0:conv_w - 1, :], axis=1)
    y = y + u * w[conv_w - 1:conv_w, :] + bdw_ref[...]
    o_ref[...] = _conv_tail(y, x1_ref[...], gcn_ref, bcn_ref, w2_ref, b2_ref, g_ref, b_ref,
                            alpha)
    ost_ref[:, 0:conv_w - 2, :] = st_ref[:, 1:conv_w - 1, :]
    ost_ref[:, conv_w - 2:conv_w - 1, :] = u[:, None, :]


def _conv_mixer(up, us, state, x1p, x1s, wdw, bdw, gcn, bcn, w2, b2, g, b, alpha):
    S, D = up.shape
    ns = us.shape[0]
    conv_w = wdw.shape[0]
    assert conv_w - 1 <= CONV_HALO
    tm = _largest_tile(S, (256, 128))
    per_halo = tm // CONV_HALO
    full = lambda shape: pl.BlockSpec(shape, lambda i: (0,) * len(shape))
    shared = [full((conv_w, D)), full((1, D)), full((1, D)), full((1, D)), full((D, D)),
              full((1, D)), full((1, D)), full((1, D))]
    shared_args = (wdw, bdw, gcn, bcn, w2, b2, g, b)
    xp = pl.pallas_call(
        functools.partial(_conv_prompt_kernel, conv_w=conv_w, alpha=alpha),
        grid=(S // tm,),
        in_specs=[pl.BlockSpec((tm, D), lambda i: (i, 0)),
                  pl.BlockSpec((CONV_HALO, D), lambda i: (jnp.maximum(i * per_halo - 1, 0), 0)),
                  pl.BlockSpec((tm, D), lambda i: (i, 0))] + shared,
        out_specs=pl.BlockSpec((tm, D), lambda i: (i, 0)),
        out_shape=jax.ShapeDtypeStruct((S, D), _F32),
        scratch_shapes=[pltpu.VMEM((CONV_HALO + tm, D), _F32), pltpu.VMEM((tm, D), _F32),
                        pltpu.VMEM((conv_w, SUBLANES, D), _F32)],
        compiler_params=_params(1),
        name="conv_prompt",
    )(up, up, x1p, *shared_args)
    tb = _largest_tile(ns, (16, 8))
    state_spec = pl.BlockSpec((None, tb, conv_w - 1, D), lambda i: (0, i, 0, 0))
    xs, new_state = pl.pallas_call(
        functools.partial(_conv_sample_kernel, conv_w=conv_w, alpha=alpha),
        grid=(ns // tb,),
        in_specs=[state_spec,
                  pl.BlockSpec((tb, D), lambda i: (i, 0)),
                  pl.BlockSpec((tb, D), lambda i: (i, 0))] + shared,
        out_specs=[pl.BlockSpec((tb, D), lambda i: (i, 0)), state_spec],
        out_shape=[jax.ShapeDtypeStruct((ns, D), _F32),
                   jax.ShapeDtypeStruct((1, ns, conv_w - 1, D), _F32)],
        compiler_params=_params(1),
        name="conv_sample",
    )(state, us, x1s, *shared_args)
    return xp, xs, new_state


def _rope_tables(pos, inv_ref, cos_ref, sin_ref):
    ang = pos * inv_ref[...]
    lane = lax.broadcasted_iota(jnp.int32, ang.shape, 1)
    cos_ref[...] = jnp.cos(ang)
    sin_ref[...] = jnp.where(lane < LANES // 2, -1.0, 1.0) * jnp.sin(ang)


def _rope_head(x, cos, sin):
    return x * cos + pltpu.roll(x, LANES // 2, axis=1) * sin


def _row_positions(i, tm):
    return (i * tm + lax.broadcasted_iota(jnp.int32, (tm, LANES), 0)).astype(_F32)


def _kv_kernel(xp_ref, xs_ref, wk_ref, wv_ref, inv_ref,
               kp_ref, vp_ref, kbp_ref, vbp_ref, km_ref, ks_ref, vs_ref,
               xb_ref, cos_ref, sin_ref, *, npt, ns, past_len):
    i, j = pl.program_id(0), pl.program_id(1)
    tn = wk_ref.shape[1]

    def body(x_ref, xb, cos_r, sin_r, pos_fn, k_ref, v_ref, kb_ref, vb_ref, kmean_ref):
        rows = x_ref.shape[0]

        @pl.when(j == 0)
        def _():
            xb[...] = x_ref[...].astype(_BF16)
            _rope_tables(pos_fn(rows), inv_ref, cos_r, sin_r)

        x16 = xb[...]
        k = jnp.dot(x16, wk_ref[...], preferred_element_type=_F32)
        v = jnp.dot(x16, wv_ref[...], preferred_element_type=_F32)
        cos, sin = cos_r[...], sin_r[...]
        k = jnp.concatenate(
            [_rope_head(k[:, h * LANES:(h + 1) * LANES], cos, sin) for h in range(tn // LANES)],
            axis=1)
        k_ref[...] = k
        v_ref[...] = v
        if kb_ref is not None:
            kb_ref[...] = k.astype(_BF16)
            nblk = rows // MOBA_BLOCK
            for b in range(nblk):
                vb_ref[b] = v[b * MOBA_BLOCK:(b + 1) * MOBA_BLOCK].T.astype(_BF16)
            means = [jnp.mean(k[b * MOBA_BLOCK:(b + 1) * MOBA_BLOCK], axis=0, keepdims=True)
                     for b in range(nblk)]
            means.append(jnp.zeros((kmean_ref.shape[0] - nblk, tn), _F32))
            kmean_ref[...] = jnp.concatenate(means, axis=0)

    _dual(i, npt,
          lambda: body(xp_ref, xb_ref, cos_ref, sin_ref, lambda rows: _row_positions(i, rows),
                       kp_ref, vp_ref, kbp_ref, vbp_ref, km_ref),
          lambda: body(xs_ref, xb_ref.at[pl.ds(0, ns)], cos_ref.at[pl.ds(0, ns)],
                       sin_ref.at[pl.ds(0, ns)],
                       lambda rows: jnp.full((rows, LANES), float(past_len), _F32),
                       ks_ref, vs_ref, None, None, None))


KMEAN_ROWS = 8


def _kv(xp, xs, w_kv, inv128, past_len, tm):
    S, D = xp.shape
    ns = xs.shape[0]
    AW = w_kv.shape[1] // 2
    tn = _largest_tile(AW, (512, 256, 128))
    npt, nj = S // tm, AW // tn
    assert tm % MOBA_BLOCK == 0 and tm // MOBA_BLOCK <= KMEAN_ROWS
    prow, srow = _prompt_tile(npt, nj), _sample_tile(npt)
    outs = pl.pallas_call(
        functools.partial(_kv_kernel, npt=npt, ns=ns, past_len=past_len),
        grid=(npt + 1, nj),
        in_specs=[pl.BlockSpec((tm, D), _prompt_rows(npt)),
                  pl.BlockSpec((ns, D), lambda i, j: (0, 0)),
                  pl.BlockSpec((D, tn), lambda i, j: (0, j)),
                  pl.BlockSpec((D, tn), lambda i, j: (0, j + nj)),
                  pl.BlockSpec((1, LANES), lambda i, j: (0, 0))],
        out_specs=[pl.BlockSpec((tm, tn), prow), pl.BlockSpec((tm, tn), prow),
                   pl.BlockSpec((tm, tn), prow),
                   pl.BlockSpec((tm // MOBA_BLOCK, tn, MOBA_BLOCK),
                                lambda i, j: (prow(i, j)[0], prow(i, j)[1], 0)),
                   pl.BlockSpec((None, KMEAN_ROWS, tn),
                                lambda i, j: (prow(i, j)[0], 0, prow(i, j)[1])),
                   pl.BlockSpec((ns, tn), srow), pl.BlockSpec((ns, tn), srow)],
        out_shape=[jax.ShapeDtypeStruct((S, AW), _F32), jax.ShapeDtypeStruct((S, AW), _F32),
                   jax.ShapeDtypeStruct((S, AW), _BF16),
                   jax.ShapeDtypeStruct((S // MOBA_BLOCK, AW, MOBA_BLOCK), _BF16),
                   jax.ShapeDtypeStruct((npt, KMEAN_ROWS, AW), _F32),
                   jax.ShapeDtypeStruct((ns, AW), _F32), jax.ShapeDtypeStruct((ns, AW), _F32)],
        scratch_shapes=[pltpu.VMEM((tm, D), _BF16), pltpu.VMEM((tm, LANES), _F32),
                        pltpu.VMEM((tm, LANES), _F32)],
        compiler_params=_params(2),
        name="kv_rope",
    )(xp, xs, w_kv, w_kv, inv128)
    kp, vp, kbp, vtp, km3, ks, vs = outs
    kmean = km3[:, :tm // MOBA_BLOCK].reshape(S // MOBA_BLOCK, AW)
    return kp, vp, kbp, vtp, kmean, ks, vs


def _select_bias(gate_t, query_block):
    nb = gate_t.shape[0]
    blk = lax.broadcasted_iota(jnp.int32, gate_t.shape, 0).astype(_F32)
    g = jnp.where(blk < query_block, gate_t, -jnp.inf)
    sel = jnp.zeros(gate_t.shape, jnp.bool_)
    for _ in range(MOBA_TOPK):
        mx = jnp.max(g, axis=0, keepdims=True)
        is_max = jnp.logical_and(g == mx, g > -jnp.inf)
        first = jnp.min(jnp.where(is_max, blk, float(nb)), axis=0, keepdims=True)
        pick = blk == first
        sel = jnp.logical_or(sel, pick)
        g = jnp.where(pick, -jnp.inf, g)
    return jnp.where(sel, 0.0, MASK_BIAS)


def _q_kernel(xp_ref, xs_ref, wq_ref, inv_ref, km_ref, qt_ref, bias_ref, qs_ref,
              xb_ref, cos_ref, sin_ref, *, npt, ns, past_len, score_scale):
    i, j = pl.program_id(0), pl.program_id(1)
    tn = wq_ref.shape[1]
    nb = km_ref.shape[0]

    def body(x_ref, xb, cos_r, sin_r, pos_fn, prompt):
        rows = x_ref.shape[0]

        @pl.when(j == 0)
        def _():
            xb[...] = x_ref[...].astype(_BF16)
            _rope_tables(pos_fn(rows), inv_ref, cos_r, sin_r)

        q = jnp.dot(xb[...], wq_ref[...], preferred_element_type=_F32)
        cos, sin = cos_r[...], sin_r[...]
        heads = [_rope_head(q[:, h * LANES:(h + 1) * LANES], cos, sin)
                 for h in range(tn // LANES)]
        q = jnp.concatenate(heads, axis=1)
        if not prompt:
            qs_ref[...] = q
            return
        qt_ref[...] = (q * score_scale).T.astype(_BF16)
        query_block = (i * rows + lax.broadcasted_iota(jnp.int32, (1, rows), 1)) // MOBA_BLOCK
        query_block = query_block.astype(_F32)
        for h, qh in enumerate(heads):
            gate_t = lax.dot_general(km_ref[:, h * LANES:(h + 1) * LANES], qh, _NT,
                                     preferred_element_type=_F32,
                                     precision=lax.Precision.HIGHEST)
            bias_ref[h * nb:(h + 1) * nb, :] = _select_bias(gate_t, query_block)

    _dual(i, npt,
          lambda: body(xp_ref, xb_ref, cos_ref, sin_ref, lambda rows: _row_positions(i, rows),
                       True),
          lambda: body(xs_ref, xb_ref.at[pl.ds(0, ns)], cos_ref.at[pl.ds(0, ns)],
                       sin_ref.at[pl.ds(0, ns)],
                       lambda rows: jnp.full((rows, LANES), float(past_len), _F32), False))


def _q_col_tile(attn_width):
    return _largest_tile(attn_width, (512, 256, 128))


def _q_proj(xp, xs, wq, inv128, kmean, past_len, tm):
    S, D = xp.shape
    ns = xs.shape[0]
    AW = wq.shape[-1]
    nb = kmean.shape[0]
    tn = _q_col_tile(AW)
    hpb = tn // LANES
    npt, nj = S // tm, AW // tn
    assert nj == 1 or (hpb * nb) % SUBLANES == 0
    prow = _prompt_tile(npt, nj)
    pcol = lambda i, j: prow(i, j)[::-1]
    return pl.pallas_call(
        functools.partial(_q_kernel, npt=npt, ns=ns, past_len=past_len,
                          score_scale=LANES ** -0.5 * math.log2(math.e)),
        grid=(npt + 1, nj),
        in_specs=[pl.BlockSpec((tm, D), _prompt_rows(npt)),
                  pl.BlockSpec((ns, D), lambda i, j: (0, 0)),
                  pl.BlockSpec((D, tn), lambda i, j: (0, j)),
                  pl.BlockSpec((1, LANES), lambda i, j: (0, 0)),
                  pl.BlockSpec((nb, tn), lambda i, j: (0, j))],
        out_specs=[pl.BlockSpec((tn, tm), pcol),
                   pl.BlockSpec((hpb * nb, tm), pcol),
                   pl.BlockSpec((ns, tn), _sample_tile(npt))],
        out_shape=[jax.ShapeDtypeStruct((AW, S), _BF16),
                   jax.ShapeDtypeStruct(((AW // LANES) * nb, S), _F32),
                   jax.ShapeDtypeStruct((ns, AW), _F32)],
        scratch_shapes=[pltpu.VMEM((tm, D), _BF16), pltpu.VMEM((tm, LANES), _F32),
                        pltpu.VMEM((tm, LANES), _F32)],
        compiler_params=_params(2),
        name="q_rope_select",
    )(xp, xs, wq, inv128, kmean)


def _attn_prompt_kernel(qt_ref, k_ref, vt_ref, bias_ref, o_ref, qa_ref, sa_ref, sb_ref, m_ref,
                        l_ref, acc_ref, *, nb):
    qi = pl.program_id(1)
    tq = qt_ref.shape[1]
    n_heads = qt_ref.shape[0] // LANES

    heads_per_slab = min(n_heads, LANES // nb)
    slabs = []
    for first in range(0, n_heads, heads_per_slab):
        slab = bias_ref[first * nb:(first + heads_per_slab) * nb, :].astype(_BF16)
        if slab.shape[0] < LANES:
            slab = jnp.concatenate(
                [slab, jnp.zeros((LANES - slab.shape[0], tq), _BF16)], axis=0)
        slabs.append(slab)
    key_i = lax.broadcasted_iota(jnp.int32, (MOBA_BLOCK, tq), 0)
    qry_i = lax.broadcasted_iota(jnp.int32, (MOBA_BLOCK, tq), 1)
    j0 = pl.multiple_of(qi * MOBA_BLOCK, MOBA_BLOCK)
    head_cols = [slice(g * LANES, (g + 1) * LANES) for g in range(n_heads)]
    scores = [jnp.dot(k_ref[pl.ds(j0, MOBA_BLOCK), cols], qt_ref[cols, :],
                      preferred_element_type=_F32) for cols in head_cols]
    probs = []
    for g in range(n_heads):
        s = jnp.where(key_i <= qry_i, scores[g], -jnp.inf)
        m = jnp.max(s, axis=0, keepdims=True)
        p = jnp.exp2(s - m)
        m_ref[g] = m
        l_ref[g] = jnp.sum(p, axis=0, keepdims=True)
        probs.append(p.astype(_BF16))
    for g, cols in enumerate(head_cols):
        acc_ref[g] = jnp.dot(vt_ref[qi, cols, :], probs[g], preferred_element_type=_F32)
        qa_ref[g] = jnp.concatenate([qt_ref[cols, :], slabs[g // heads_per_slab]], axis=0)

    lane = lax.broadcasted_iota(jnp.int32, (MOBA_BLOCK, LANES), 1)

    def past_scores(j):
        k0 = pl.multiple_of(j * MOBA_BLOCK, MOBA_BLOCK)
        out = []
        for g, cols in enumerate(head_cols):
            onehot = jnp.where(lane == (g % heads_per_slab) * nb + j, 1.0, 0.0).astype(_BF16)
            k_aug = jnp.concatenate([k_ref[pl.ds(k0, MOBA_BLOCK), cols], onehot], axis=1)
            out.append(jnp.dot(k_aug, qa_ref[g], preferred_element_type=_F32))
        return out

    def compute_scores(j, s_ref):
        for g, s in enumerate(past_scores(j)):
            s_ref[g] = s

    def consume_scores(j, s_ref):
        probs, corrs = [], []
        for g in range(n_heads):
            s = s_ref[g]
            m_old = m_ref[g]
            m_new = jnp.maximum(m_old, jnp.max(s, axis=0, keepdims=True))
            corr = jnp.exp2(m_old - m_new)
            pj = jnp.exp2(s - m_new)
            l_ref[g] = corr * l_ref[g] + jnp.sum(pj, axis=0, keepdims=True)
            m_ref[g] = m_new
            probs.append(pj.astype(_BF16))
            corrs.append(corr)
        for g, cols in enumerate(head_cols):
            acc_ref[g] = corrs[g] * acc_ref[g] + jnp.dot(vt_ref[j, cols, :], probs[g],
                                                         preferred_element_type=_F32)

    last = nb - 1
    compute_scores(0, sa_ref)

    def step(t, carry):
        j = 2 * t
        compute_scores(jnp.minimum(j + 1, last), sb_ref)
        consume_scores(j, sa_ref)
        compute_scores(jnp.minimum(j + 2, last), sa_ref)
        consume_scores(jnp.minimum(j + 1, last), sb_ref)
        return carry

    lax.fori_loop(0, (qi + 1) // 2, step, 0)
    for g in range(n_heads):
        out_t = acc_ref[g] / l_ref[g]
        o_ref[:, g * LANES:(g + 1) * LANES] = out_t.T.astype(o_ref.dtype)


ATTN_HEADS_PER_STEP = 4


def _attn_prompt(qt, kb, vt, bias_t, head_dim, nb):
    AW, S = qt.shape
    assert head_dim == LANES and nb <= LANES
    tq = MOBA_BLOCK
    hpb = _largest_tile(AW // head_dim, (ATTN_HEADS_PER_STEP, 4, 2, 1))
    assert hpb % min(hpb, LANES // nb) == 0
    gw = hpb * head_dim
    return pl.pallas_call(
        functools.partial(_attn_prompt_kernel, nb=nb),
        grid=(AW // gw, S // tq),
        in_specs=[pl.BlockSpec((gw, tq), lambda h, i: (h, i)),
                  pl.BlockSpec((S, gw), lambda h, i: (0, h)),
                  pl.BlockSpec((nb, gw, MOBA_BLOCK), lambda h, i: (0, h, 0)),
                  pl.BlockSpec((hpb * nb, tq), lambda h, i: (h, i))],
        out_specs=pl.BlockSpec((tq, gw), lambda h, i: (i, h)),
        out_shape=jax.ShapeDtypeStruct((S, AW), _BF16),
        scratch_shapes=[pltpu.VMEM((hpb, 2 * LANES, tq), _BF16),
                        pltpu.VMEM((hpb, MOBA_BLOCK, tq), _F32),
                        pltpu.VMEM((hpb, MOBA_BLOCK, tq), _F32),
                        pltpu.VMEM((hpb, 1, tq), _F32), pltpu.VMEM((hpb, 1, tq), _F32),
                        pltpu.VMEM((hpb, head_dim, tq), _F32)],
        compiler_params=_params(2),
        name="moba_prompt",
    )(qt, kb, vt, bias_t)


PAGES_PER_BLOCK = 2


def _attn_sample_kernel(pt_ref, q_ref, kn_ref, vn_ref, *refs, n_blocks, blocks_per_step,
                        scale_log2e):
    del pt_ref
    n_pages = PAGES_PER_BLOCK * blocks_per_step
    k_refs, v_refs = refs[:n_pages], refs[n_pages:2 * n_pages]
    o_ref, m_ref, l_ref, g_ref, acc_ref = refs[2 * n_pages:]
    step = pl.program_id(1)
    page, H, Dh = k_refs[0].shape
    rows = page * H
    q = q_ref[...]
    q_hi = q.astype(_BF16)
    q_lo = (q - q_hi.astype(_F32)).astype(_BF16)
    q2 = jnp.concatenate([q_hi, q_lo], axis=0)

    def scores(k_ref):
        k16 = k_ref[...].reshape(rows, Dh).astype(_BF16)
        s2 = lax.dot_general(q2, k16, _NT, preferred_element_type=_F32)
        return s2[0:H] + s2[H:2 * H]

    head_of_row = lax.broadcasted_iota(jnp.int32, (H, PAGES_PER_BLOCK * rows), 1) % H
    own = head_of_row == lax.broadcasted_iota(jnp.int32, (H, PAGES_PER_BLOCK * rows), 0)
    for t in range(blocks_per_step):
        k0_ref, k1_ref = k_refs[2 * t], k_refs[2 * t + 1]
        v0_ref, v1_ref = v_refs[2 * t], v_refs[2 * t + 1]
        s = jnp.concatenate([scores(k0_ref), scores(k1_ref)], axis=1)
        sm = jnp.where(own, s, -jnp.inf)
        m = jnp.max(sm, axis=1, keepdims=True)
        p = jnp.exp2((sm - m) * scale_log2e)
        l = jnp.sum(p, axis=1, keepdims=True)
        g = jnp.sum(jnp.where(own, s, 0.0), axis=1, keepdims=True)
        p16 = p.astype(_BF16)
        o = (jnp.dot(p16[:, 0:rows], v0_ref[...].reshape(rows, Dh).astype(_BF16),
                     preferred_element_type=_F32)
             + jnp.dot(p16[:, rows:2 * rows], v1_ref[...].reshape(rows, Dh).astype(_BF16),
                       preferred_element_type=_F32))
        n = step * blocks_per_step + t
        m_ref[n] = jnp.broadcast_to(m, (H, LANES))
        l_ref[n] = jnp.broadcast_to(l, (H, LANES))
        g_ref[n] = jnp.broadcast_to(g, (H, LANES))
        acc_ref[n] = o

    @pl.when(step == n_blocks // blocks_per_step - 1)
    def _():
        gates = [g_ref[b] for b in range(n_blocks)]
        s_own = jnp.sum(q * kn_ref[...], axis=1, keepdims=True)
        m_tot = jnp.broadcast_to(s_own, (H, LANES))
        sel = []
        for b in range(n_blocks):
            rank = jnp.zeros((H, LANES), _F32)
            for c in range(n_blocks):
                if c == b:
                    continue
                ahead = gates[c] >= gates[b] if c < b else gates[c] > gates[b]
                rank = rank + jnp.where(ahead, 1.0, 0.0)
            sel.append(rank < float(MOBA_TOPK))
            m_tot = jnp.where(sel[b], jnp.maximum(m_tot, m_ref[b]), m_tot)
        w_own = jnp.exp2((s_own - m_tot) * scale_log2e)
        den = w_own
        num = w_own * vn_ref[...]
        for b in range(n_blocks):
            w = jnp.where(sel[b], jnp.exp2((m_ref[b] - m_tot) * scale_log2e), 0.0)
            den = den + w * l_ref[b]
            num = num + w * acc_ref[b]
        o_ref[...] = num / den


def _attn_sample(q3, kn3, vn3, cache_k, cache_v, page_table, past_len):
    Bd, H, Dh = q3.shape
    page = cache_k.shape[1]
    assert Dh == LANES and MOBA_BLOCK == PAGES_PER_BLOCK * page
    assert past_len % MOBA_BLOCK == 0
    n_blocks = past_len // MOBA_BLOCK
    assert n_blocks >= MOBA_TOPK
    bps = _largest_tile(n_blocks, (4, 2, 1))
    pps = PAGES_PER_BLOCK * bps
    n_pages = page_table.shape[1]
    pt = page_table.reshape(-1)
    tok = pl.BlockSpec((None, H, Dh), lambda b, n, pt: (b, 0, 0))

    def page_spec(which):
        return pl.BlockSpec((None, page, H, Dh),
                            lambda b, n, pt: (pt[b * n_pages + pps * n + which], 0, 0, 0))

    pages = [page_spec(w) for w in range(pps)]
    grid_spec = pltpu.PrefetchScalarGridSpec(
        num_scalar_prefetch=1,
        grid=(Bd, n_blocks // bps),
        in_specs=[tok, tok, tok] + pages + pages,
        out_specs=tok,
        scratch_shapes=[pltpu.VMEM((n_blocks, H, LANES), _F32),
                        pltpu.VMEM((n_blocks, H, LANES), _F32),
                        pltpu.VMEM((n_blocks, H, LANES), _F32),
                        pltpu.VMEM((n_blocks, H, Dh), _F32)],
    )
    return pl.pallas_call(
        functools.partial(_attn_sample_kernel, n_blocks=n_blocks, blocks_per_step=bps,
                          scale_log2e=Dh ** -0.5 * math.log2(math.e)),
        grid_spec=grid_spec,
        out_shape=jax.ShapeDtypeStruct((Bd, H, Dh), _F32),
        compiler_params=_params(2),
        name="moba_sample",
    )(pt, q3, kn3, vn3, *([cache_k] * pps), *([cache_v] * pps))


def _oproj_kernel(ap_ref, as_ref, xp_ref, xs_ref, wo_ref, g_ref, b_ref, op_ref, os_ref,
                  *, npt, alpha):
    i = pl.program_id(0)

    def body(a_ref, x_ref, o_ref):
        m = jnp.dot(a_ref[...].astype(_BF16), wo_ref[...], preferred_element_type=_F32)
        o_ref[...] = _layer_norm(alpha * x_ref[...] + m, g_ref[...], b_ref[...])

    _dual(i, npt, lambda: body(ap_ref, xp_ref, op_ref), lambda: body(as_ref, xs_ref, os_ref))


def _oproj(ap, a_s, xp, xs, wo, g4, b4, layer, alpha, tm):
    S, D = xp.shape
    ns = xs.shape[0]
    AW = ap.shape[1]
    npt = S // tm
    vec = pl.BlockSpec((None, None, 1, D), lambda i: (layer, 1, 0, 0))
    return pl.pallas_call(
        functools.partial(_oproj_kernel, npt=npt, alpha=alpha),
        grid=(npt + 1,),
        in_specs=[pl.BlockSpec((tm, AW), _prompt_rows(npt)),
                  pl.BlockSpec((ns, AW), lambda i: (0, 0)),
                  pl.BlockSpec((tm, D), _prompt_rows(npt)),
                  pl.BlockSpec((ns, D), lambda i: (0, 0)),
                  pl.BlockSpec((AW, D), lambda i: (0, 0)), vec, vec],
        out_specs=[pl.BlockSpec((tm, D), _prompt_rows(npt)),
                   pl.BlockSpec((ns, D), lambda i: (0, 0))],
        out_shape=[jax.ShapeDtypeStruct((S, D), _F32), jax.ShapeDtypeStruct((ns, D), _F32)],
        compiler_params=_params(1),
        name="oproj_postnorm",
    )(ap, a_s, xp, xs, wo, g4, b4)


def kernel(x_prompt, x_sample, state_conv, cache_k, cache_v, page_table, ffn_w_gate, ffn_w_up, ffn_w_down, ln_g, ln_b, conv_w_pw1, conv_b_pw1, conv_w_dw, conv_b_dw, conv_ln_g, conv_ln_b, conv_w_pw2, conv_b_pw2, attn_w_q, attn_w_o, w_kv):
    B, S, D = x_prompt.shape
    Bd, Sd, _ = x_sample.shape
    depth = ffn_w_gate.shape[0]
    n_phys, page, H, Dh = cache_k.shape
    past_len = page_table.shape[1] * page
    conv_w = conv_w_dw.shape[1]
    assert B == 1 and Sd == 1 and depth == 2 and conv_w_pw1.shape[0] == 1
    assert S % MOBA_BLOCK == 0 and H * Dh == attn_w_q.shape[-1]
    alpha = (2.0 * depth) ** 0.25
    tm = _largest_tile(S, (512, 256))
    tm_ffn = _largest_tile(S, (1024, 512, 256))

    ffn_f32 = (ffn_w_gate, ffn_w_up, ffn_w_down)
    w16 = tuple(w[0, 0].astype(_BF16) for w in ffn_f32)
    g4, b4 = ln_g[:, :, None, :], ln_b[:, :, None, :]
    half = Dh // 2
    inv = ROPE_THETA ** (-jnp.arange(half, dtype=_F32) / half)
    inv128 = jnp.concatenate([inv, inv])[None, :]

    xp, xs = x_prompt.reshape(S, D), x_sample.reshape(Bd, D)

    xp, xs, w16, (w_pw1, w_pw2) = _ffn(xp, xs, w16, g4, b4, 0, 0, alpha, tm_ffn,
                                       ffn_f32 + (0, 1), (conv_w_pw1[0], conv_w_pw2[0]))
    up, us = _glu(xp, xs, w_pw1, conv_b_pw1, tm_ffn)
    xp, xs, conv_s = _conv_mixer(up, us, state_conv, xp, xs, conv_w_dw[0], conv_b_dw,
                                 conv_ln_g, conv_ln_b, w_pw2, conv_b_pw2,
                                 ln_g[0, 1:2], ln_b[0, 1:2], alpha)
    xp, xs, w16, (w_kv16,) = _ffn(xp, xs, w16, g4, b4, 0, 1, alpha, tm_ffn, ffn_f32 + (1, 0),
                                  (w_kv,))
    kp, vp, kbp, vt, kmean, ks, vs = _kv(xp, xs, w_kv16, inv128, past_len, tm_ffn)

    xp, xs, w16, (w_q, w_o) = _ffn(xp, xs, w16, g4, b4, 1, 0, alpha, tm_ffn, ffn_f32 + (1, 1),
                                   (attn_w_q[0], attn_w_o[0]))
    nb = S // MOBA_BLOCK
    qt, bias_t, qs = _q_proj(xp, xs, w_q, inv128, kmean, past_len, tm_ffn)
    ap = _attn_prompt(qt, kbp, vt, bias_t, Dh, nb)
    a_s = _attn_sample(qs.reshape(Bd, H, Dh), ks.reshape(Bd, H, Dh), vs.reshape(Bd, H, Dh),
                       cache_k, cache_v, page_table, past_len)
    xp, xs = _oproj(ap, a_s.reshape(Bd, H * Dh), xp, xs, w_o, g4, b4, 1, alpha, tm)
    xp, xs, _, _ = _ffn(xp, xs, w16, g4, b4, 1, 1, alpha, tm_ffn)

    conv_p = up[S - (conv_w - 1):].reshape(1, 1, conv_w - 1, D)
    return (xp.reshape(1, S, D), xs.reshape(Bd, 1, D), conv_p, conv_s,
            kp.reshape(1, S, H, Dh), vp.reshape(1, S, H, Dh),
            ks.reshape(Bd, 1, H, Dh), vs.reshape(Bd, 1, H, Dh))
```
